```python
import math
import jax, jax.numpy as jnp
from jax import lax
import numpy as np

D_MODEL = 2048
BATCH = 4
SEQ = 4096
DEPTH = 1
DEC_BATCH = 16
DEC_SEQ = 2048
PAST_LEN = 128

SSM_WIDTH = 1024
SSM_GROUP = 16
SSM_GROUPS = SSM_WIDTH // SSM_GROUP
SSM_STATE = 64
ATT_HEADS = 4
ATT_HEAD_DIM = 128
ATT_V_DIM = 2 * ATT_HEAD_DIM
ATT_QK_WIDTH = ATT_HEADS * 2 * ATT_HEAD_DIM
ATT_WIDTH = ATT_HEADS * ATT_V_DIM
Q_BLOCK = 128
NUM_BUCKETS = 32
MAX_DISTANCE = 128
PLE_DIM = 256
IN_WIDTH = 2 * SSM_WIDTH + 2 * ATT_QK_WIDTH + 2 * ATT_WIDTH + 2 * D_MODEL
EPS = 1e-6

kernel_name = "hybrid_s5_diffattn_encoder"


def rmsnorm(x, g):
    xf = x.astype(jnp.float32)
    y = xf * lax.rsqrt(jnp.mean(xf * xf, axis=-1, keepdims=True) + EPS) * g.astype(jnp.float32)
    return y.astype(x.dtype)


def rel_bucket(rel):
    half = NUM_BUCKETS // 2
    max_exact = half // 2
    ret = (rel > 0).astype(jnp.int32) * half
    n = jnp.abs(rel)
    nf = jnp.maximum(n, 1).astype(jnp.float32)
    large = max_exact + (jnp.log(nf / max_exact) / math.log(MAX_DISTANCE / max_exact)
                         * (half - max_exact)).astype(jnp.int32)
    large = jnp.minimum(large, half - 1)
    return ret + jnp.where(n < max_exact, n, large)


def _scan_combine(e1, e2):
    a1r, a1i, b1r, b1i = e1
    a2r, a2i, b2r, b2i = e2
    ar = a1r * a2r - a1i * a2i
    ai = a1r * a2i + a1i * a2r
    br = a2r * b1r - a2i * b1i + b2r
    bi = a2r * b1i + a2i * b1r + b2i
    return ar, ai, br, bi


def s5_direction(u, lam_re, lam_im, log_dt, b_re, b_im, c_re, c_im, reverse):
    lam_re = lam_re.astype(jnp.float32)
    lam_im = lam_im.astype(jnp.float32)
    dt = jnp.exp(log_dt.astype(jnp.float32))[:, None]
    mag = jnp.exp(lam_re * dt)
    ab_re = mag * jnp.cos(lam_im * dt)
    ab_im = mag * jnp.sin(lam_im * dt)
    den = lam_re * lam_re + lam_im * lam_im
    nr = ab_re - 1.0
    ni = ab_im
    k_re = (nr * lam_re + ni * lam_im) / den
    k_im = (ni * lam_re - nr * lam_im) / den
    b_re = b_re.astype(jnp.float32)
    b_im = b_im.astype(jnp.float32)
    bb_re = k_re[..., None] * b_re - k_im[..., None] * b_im
    bb_im = k_re[..., None] * b_im + k_im[..., None] * b_re
    if reverse:
        u = jnp.flip(u, axis=1)
    br = jnp.einsum('blgc,gnc->lbgn', u, bb_re)
    bi = jnp.einsum('blgc,gnc->lbgn', u, bb_im)
    l = u.shape[1]
    ar = jnp.broadcast_to(ab_re, (l, 1) + ab_re.shape)
    ai = jnp.broadcast_to(ab_im, (l, 1) + ab_im.shape)
    _, _, hr, hi = lax.associative_scan(_scan_combine, (ar, ai, br, bi), axis=0)
    y = (jnp.einsum('lbgn,gcn->blgc', hr, c_re.astype(jnp.float32))
         - jnp.einsum('lbgn,gcn->blgc', hi, c_im.astype(jnp.float32)))
    if reverse:
        y = jnp.flip(y, axis=1)
    return y


def diff_attention(q, k, v, rel_bias, lam):
    b, l = q.shape[0], q.shape[1]
    nb = l // Q_BLOCK
    scale = ATT_HEAD_DIM ** -0.5
    qb = q.reshape(b, nb, Q_BLOCK, ATT_HEADS, 2, ATT_HEAD_DIM).swapaxes(0, 1)
    starts = jnp.arange(nb, dtype=jnp.int32) * Q_BLOCK
    kpos = jnp.arange(l, dtype=jnp.int32)
    vf = v.astype(jnp.float32)

    def block(args):
        qi, start = args
        qpos = start + jnp.arange(Q_BLOCK, dtype=jnp.int32)
        bias = rel_bias[rel_bucket(kpos[None, :] - qpos[:, None])].astype(jnp.float32)
        bias = jnp.transpose(bias, (2, 0, 1))
        s = jnp.einsum('bqhcd,bkhcd->bhcqk', qi, k).astype(jnp.float32) * scale + bias[None, :, None]
        pr = jax.nn.softmax(s, axis=-1)
        w = pr[:, :, 0] - lam * pr[:, :, 1]
        return jnp.einsum('bhqk,bkhe->bqhe', w, vf)

    o = lax.map(block, (qb, starts))
    return o.swapaxes(0, 1).reshape(b, l, ATT_HEADS, ATT_V_DIM)


def trunk(x, p, rel_bias, norm_g, w_in, ssm_lambda_re, ssm_lambda_im, ssm_log_dt,
          ssm_b_re, ssm_b_im, ssm_c_re, ssm_c_im, ssm_d, glu_w, glu_b,
          lam_q1, lam_k1, lam_q2, lam_k2, subln_g, w_branch_s, w_branch_a, w_out,
          ple_norm_g, ple_gate_w, ple_proj_w, final_g):
    b, l = x.shape[0], x.shape[1]
    h = x
    sizes = (SSM_WIDTH, SSM_WIDTH, ATT_QK_WIDTH, ATT_QK_WIDTH, ATT_WIDTH, ATT_WIDTH, D_MODEL, D_MODEL)
    cuts = [int(c) for c in np.cumsum(sizes)[:-1]]
    for i in range(DEPTH):
        hn = rmsnorm(h, norm_g[i])
        proj = hn @ w_in[i]
        s_x, s_z, q, k, v, a_z, g_s, g_a = jnp.split(proj, cuts, axis=-1)

        u = s_x.astype(jnp.float32).reshape(b, l, SSM_GROUPS, SSM_GROUP)
        y = (s5_direction(u, ssm_lambda_re[i, 0], ssm_lambda_im[i, 0], ssm_log_dt[i, 0],
                          ssm_b_re[i, 0], ssm_b_im[i, 0], ssm_c_re[i, 0], ssm_c_im[i, 0], False)
             + s5_direction(u, ssm_lambda_re[i, 1], ssm_lambda_im[i, 1], ssm_log_dt[i, 1],
                            ssm_b_re[i, 1], ssm_b_im[i, 1], ssm_c_re[i, 1], ssm_c_im[i, 1], True)
             + ssm_d[i].astype(jnp.float32).reshape(SSM_GROUPS, SSM_GROUP) * u)
        y = jax.nn.gelu(y.reshape(b, l, SSM_WIDTH))
        y = y * jax.nn.sigmoid(y @ glu_w[i].astype(jnp.float32) + glu_b[i].astype(jnp.float32))
        y = (y * jax.nn.silu(s_z.astype(jnp.float32))).astype(x.dtype)
        y_s = y @ w_branch_s[i]

        lam_init = 0.8 - 0.6 * math.exp(-0.3 * i)
        lam = (jnp.exp(jnp.sum(lam_q1[i].astype(jnp.float32) * lam_k1[i].astype(jnp.float32)))
               - jnp.exp(jnp.sum(lam_q2[i].astype(jnp.float32) * lam_k2[i].astype(jnp.float32)))
               + lam_init)
        qh = q.reshape(b, l, ATT_HEADS, 2, ATT_HEAD_DIM)
        kh = k.reshape(b, l, ATT_HEADS, 2, ATT_HEAD_DIM)
        vh = v.reshape(b, l, ATT_HEADS, ATT_V_DIM)
        o = diff_attention(qh, kh, vh, rel_bias, lam)
        o = rmsnorm(o, subln_g[i]) * (1.0 - lam_init)
        o = (o.reshape(b, l, ATT_WIDTH) * jax.nn.silu(a_z.astype(jnp.float32))).astype(x.dtype)
        y_a = o @ w_branch_a[i]

        merged = jax.nn.sigmoid(g_s) * y_s + jax.nn.sigmoid(g_a) * y_a
        h = h + (merged @ w_out[i]).astype(h.dtype)

        gate = jax.nn.sigmoid(rmsnorm(h, ple_norm_g[i]) @ ple_gate_w[i])
        h = h + (gate * (p[i] @ ple_proj_w[i])).astype(h.dtype)
    return rmsnorm(h, final_g)


def setup_inputs(seed: int = 0) -> dict:
    key = jax.random.key(seed)
    ks = jax.random.split(key, 32)
    f32 = jnp.float32
    nrm = lambda k, s, sc: jax.random.normal(k, s, f32) * sc
    n_idx = jnp.arange(SSM_STATE, dtype=f32)
    lam_re = -0.5 + nrm(ks[8], (DEPTH, 2, SSM_GROUPS, SSM_STATE), 0.01)
    lam_im = math.pi * n_idx + nrm(ks[9], (DEPTH, 2, SSM_GROUPS, SSM_STATE), 0.01)
    log_dt = jax.random.uniform(ks[10], (DEPTH, 2, SSM_GROUPS), f32, math.log(1e-3), math.log(1e-1))
    return {
        "x_prompt": nrm(ks[0], (BATCH, SEQ, D_MODEL), 1.0),
        "x_sample": nrm(ks[1], (DEC_BATCH, DEC_SEQ, D_MODEL), 1.0),
        "p_prompt": nrm(ks[2], (DEPTH, BATCH, SEQ, PLE_DIM), 1.0),
        "p_sample": nrm(ks[3], (DEPTH, DEC_BATCH, DEC_SEQ, PLE_DIM), 1.0),
        "rel_bias": nrm(ks[4], (NUM_BUCKETS, ATT_HEADS), 0.5),
        "norm_g": 1.0 + nrm(ks[5], (DEPTH, D_MODEL), 0.02),
        "w_in": nrm(ks[6], (DEPTH, D_MODEL, IN_WIDTH), D_MODEL ** -0.5),
        "ssm_lambda_re": lam_re,
        "ssm_lambda_im": lam_im,
        "ssm_log_dt": log_dt,
        "ssm_b_re": nrm(ks[11], (DEPTH, 2, SSM_GROUPS, SSM_STATE, SSM_GROUP), (2 * SSM_GROUP) ** -0.5),
        "ssm_b_im": nrm(ks[12], (DEPTH, 2, SSM_GROUPS, SSM_STATE, SSM_GROUP), (2 * SSM_GROUP) ** -0.5),
        "ssm_c_re": nrm(ks[13], (DEPTH, 2, SSM_GROUPS, SSM_GROUP, SSM_STATE), (2 * SSM_STATE) ** -0.5),
        "ssm_c_im": nrm(ks[14], (DEPTH, 2, SSM_GROUPS, SSM_GROUP, SSM_STATE), (2 * SSM_STATE) ** -0.5),
        "ssm_d": nrm(ks[15], (DEPTH, SSM_WIDTH), 1.0),
        "glu_w": nrm(ks[16], (DEPTH, SSM_WIDTH, SSM_WIDTH), SSM_WIDTH ** -0.5),
        "glu_b": nrm(ks[17], (DEPTH, SSM_WIDTH), 0.02),
        "lam_q1": nrm(ks[18], (DEPTH, ATT_HEAD_DIM), 0.1),
        "lam_k1": nrm(ks[19], (DEPTH, ATT_HEAD_DIM), 0.1),
        "lam_q2": nrm(ks[20], (DEPTH, ATT_HEAD_DIM), 0.1),
        "lam_k2": nrm(ks[21], (DEPTH, ATT_HEAD_DIM), 0.1),
        "subln_g": 1.0 + nrm(ks[22], (DEPTH, ATT_V_DIM), 0.02),
        "w_branch_s": nrm(ks[23], (DEPTH, SSM_WIDTH, D_MODEL), SSM_WIDTH ** -0.5),
        "w_branch_a": nrm(ks[24], (DEPTH, ATT_WIDTH, D_MODEL), ATT_WIDTH ** -0.5),
        "w_out": nrm(ks[25], (DEPTH, D_MODEL, D_MODEL), D_MODEL ** -0.5),
        "ple_norm_g": 1.0 + nrm(ks[26], (DEPTH, D_MODEL), 0.02),
        "ple_gate_w": nrm(ks[27], (DEPTH, D_MODEL, D_MODEL), D_MODEL ** -0.5),
        "ple_proj_w": nrm(ks[28], (DEPTH, PLE_DIM, D_MODEL), PLE_DIM ** -0.5),
        "final_g": 1.0 + nrm(ks[29], (D_MODEL,), 0.02),
    }


def reference(x_prompt, x_sample, p_prompt, p_sample, rel_bias, norm_g, w_in,
              ssm_lambda_re, ssm_lambda_im, ssm_log_dt, ssm_b_re, ssm_b_im, ssm_c_re, ssm_c_im,
              ssm_d, glu_w, glu_b, lam_q1, lam_k1, lam_q2, lam_k2, subln_g,
              w_branch_s, w_branch_a, w_out, ple_norm_g, ple_gate_w, ple_proj_w, final_g):
    weights = (rel_bias, norm_g, w_in, ssm_lambda_re, ssm_lambda_im, ssm_log_dt,
               ssm_b_re, ssm_b_im, ssm_c_re, ssm_c_im, ssm_d, glu_w, glu_b,
               lam_q1, lam_k1, lam_q2, lam_k2, subln_g, w_branch_s, w_branch_a, w_out,
               ple_norm_g, ple_gate_w, ple_proj_w, final_g)
    y_prompt = trunk(x_prompt, p_prompt, *weights)
    y_sample = trunk(x_sample, p_sample, *weights)
    return (y_prompt, y_sample)
```

```python
import functools
import math

import jax
import jax.numpy as jnp
import numpy as np
from jax import lax
from jax.experimental import pallas as pl
from jax.experimental.pallas import tpu as pltpu

F32 = jnp.float32
BF16 = jnp.bfloat16

D_MODEL = 2048
SSM_WIDTH = 1024
SSM_GROUP = 16
SSM_GROUPS = SSM_WIDTH // SSM_GROUP
SSM_STATE = 64
ATT_HEADS = 4
ATT_HEAD_DIM = 128
ATT_V_DIM = 2 * ATT_HEAD_DIM
ATT_QK_WIDTH = ATT_HEADS * 2 * ATT_HEAD_DIM
ATT_WIDTH = ATT_HEADS * ATT_V_DIM
NUM_BUCKETS = 32
MAX_DISTANCE = 128
PLE_DIM = 256
EPS = 1e-6
LAM_INIT = 0.8 - 0.6 * math.exp(-0.3 * 0)
ATT_SCALE = ATT_HEAD_DIM ** -0.5

LANES = 128
SUBLANES = 8
SLABS = SSM_WIDTH // LANES
SLAB_GROUPS = LANES // SSM_GROUP
S5_CHUNK = 8
SLAB_STATE = SLAB_GROUPS * SSM_STATE
S5_K = S5_CHUNK * LANES
S5_ROWS = 512

IN_WIDTH = 2 * SSM_WIDTH + 2 * ATT_QK_WIDTH + 2 * ATT_WIDTH + 2 * D_MODEL
PROJ_W = IN_WIDTH - SSM_WIDTH
COL_GATES = 0
COL_SZ = 4096
COL_AZ = 5120
COL_Q = 6144
COL_K = 7168
COL_V = 8192
IN_TN = 1024
Q_TILE_INDEX = COL_Q // IN_TN
VMEM_LIMIT = 56 * 1024 * 1024


def _cparams(sem):
    return pltpu.CompilerParams(dimension_semantics=sem, vmem_limit_bytes=VMEM_LIMIT)


def _dot(a, b):
    return jnp.dot(a, b, preferred_element_type=F32)


def _const_spec(shape):
    zeros = (0,) * len(shape)
    return pl.BlockSpec(shape, lambda *_: zeros, pipeline_mode=pl.Buffered(1))


def _inproj_kernel(x_ref, g_ref, w_ref, proj_ref, sx_ref, hn_ref):
    j = pl.program_id(1)
    last = pl.num_programs(1) - 1

    @pl.when(j == 0)
    def _():
        x = x_ref[...]
        ms = jnp.mean(x * x, axis=-1, keepdims=True)
        hn_ref[...] = (x * lax.rsqrt(ms + EPS) * g_ref[...]).astype(BF16)

    acc = _dot(hn_ref[...], w_ref[...])

    @pl.when(j < last)
    def _():
        scale = jnp.where(j == Q_TILE_INDEX, ATT_SCALE, 1.0).astype(F32)
        proj_ref[...] = (acc * scale).astype(BF16)

    @pl.when(j == last)
    def _():
        for s in range(SLABS):
            sx_ref[s] = acc[:, s * LANES:(s + 1) * LANES]


def _inproj(x2d, norm_g, w_in, tm):
    t = x2d.shape[0]
    nj = w_in.shape[1] // IN_TN
    return pl.pallas_call(
        _inproj_kernel,
        grid=(t // tm, nj),
        in_specs=[
            pl.BlockSpec((tm, D_MODEL), lambda i, j: (i, 0)),
            pl.BlockSpec((1, D_MODEL), lambda i, j: (0, 0)),
            pl.BlockSpec((D_MODEL, IN_TN), lambda i, j: (0, j)),
        ],
        out_specs=[
            pl.BlockSpec((tm, IN_TN), lambda i, j: (i, jnp.minimum(j, nj - 2))),
            pl.BlockSpec((SLABS, tm, LANES), lambda i, j: (0, i, 0)),
        ],
        out_shape=[
            jax.ShapeDtypeStruct((t, PROJ_W), BF16),
            jax.ShapeDtypeStruct((SLABS, t, LANES), F32),
        ],
        scratch_shapes=[pltpu.VMEM((tm, D_MODEL), BF16)],
        compiler_params=_cparams(("parallel", "arbitrary")),
        name="inproj",
    )(x2d, norm_g, w_in)


def _s5_kernel(xf_ref, xb_ref, wsf_ref, wsb_ref, wi_ref, wof_ref, wob_ref, af_ref, ab_ref,
               yf_ref, yb_ref, xs_ref, sf_ref, sb_ref, ys_ref, carry_ref, *, nb, rb):
    i = pl.program_id(2)
    half = SLAB_STATE

    @pl.when(i == 0)
    def _():
        carry_ref[...] = jnp.zeros_like(carry_ref)

    def gather_rows(x_ref):
        for j in range(S5_CHUNK):
            for b in range(nb):
                xs_ref[j, pl.ds(b, rb, stride=nb), :] = x_ref[b, pl.ds(j, rb, stride=S5_CHUNK), :]
        return jnp.concatenate([xs_ref[j] for j in range(S5_CHUNK)], axis=-1).astype(BF16)

    def scatter_rows(y, y_ref):
        for j in range(S5_CHUNK):
            ys_ref[j] = y[:, j * LANES:(j + 1) * LANES]
        for j in range(S5_CHUNK):
            for b in range(nb):
                y_ref[b, pl.ds(j, rb, stride=S5_CHUNK), :] = ys_ref[j, pl.ds(b, rb, stride=nb), :]

    def scan(s_ref, a_ref, slot, reverse):
        a_re = jnp.broadcast_to(a_ref[:, :half], (nb, half))
        a_im = jnp.broadcast_to(a_ref[:, half:], (nb, half))

        per = max(1, SUBLANES // nb)
        rows = per * nb
        steps = rb // per

        def body(step, carry):
            e_re, e_im = carry
            k = (steps - 1 - step) if reverse else step
            row = pl.multiple_of(k * rows, rows)
            s = s_ref[pl.ds(row, rows), :]
            entering = [None] * per
            for q in (range(per - 1, -1, -1) if reverse else range(per)):
                sq = s[q * nb:(q + 1) * nb]
                entering[q] = jnp.concatenate([e_re, e_im], axis=-1)
                e_re, e_im = (a_re * e_re - a_im * e_im + sq[:, :half],
                              a_re * e_im + a_im * e_re + sq[:, half:])
            s_ref[pl.ds(row, rows), :] = jnp.concatenate(entering, axis=0)
            return e_re, e_im

        c0 = carry_ref[slot]
        e_re, e_im = lax.fori_loop(0, steps, body, (c0[:, :half], c0[:, half:]), unroll=4)
        carry_ref[slot] = jnp.concatenate([e_re, e_im], axis=-1)

    x_f = gather_rows(xf_ref)
    sf_ref[...] = _dot(x_f, wsf_ref[...])
    x_b = gather_rows(xb_ref)
    sb_ref[...] = _dot(x_b, wsb_ref[...])
    scan(sf_ref, af_ref, 0, False)
    scan(sb_ref, ab_ref, 1, True)
    y_f = _dot(x_f, wi_ref[...]) + _dot(sf_ref[...].astype(BF16), wof_ref[...])
    scatter_rows(y_f, yf_ref)
    y_b = _dot(sb_ref[...].astype(BF16), wob_ref[...])
    scatter_rows(y_b, yb_ref)


def _s5(sx, s5w, bt, l, nb):
    lb = min(S5_ROWS * S5_CHUNK // nb, l)
    rb = lb // S5_CHUNK
    nl = l // lb
    sx4 = sx.reshape(SLABS, bt, l, LANES)
    wsf, wsb, wi, wof, wob, a_f, a_b = s5w
    wspec = pl.BlockSpec((None, S5_K, S5_K), lambda s, g, i: (s, 0, 0))
    aspec = pl.BlockSpec((None, 1, 2 * SLAB_STATE), lambda s, g, i: (s, 0, 0))
    fwd = pl.BlockSpec((None, nb, lb, LANES), lambda s, g, i: (s, g, i, 0))
    bwd = pl.BlockSpec((None, nb, lb, LANES), lambda s, g, i: (s, g, nl - 1 - i, 0))
    m = rb * nb
    yf, yb = pl.pallas_call(
        functools.partial(_s5_kernel, nb=nb, rb=rb),
        grid=(SLABS, bt // nb, nl),
        in_specs=[fwd, bwd, wspec, wspec, wspec, wspec, wspec, aspec, aspec],
        out_specs=[fwd, bwd],
        out_shape=[jax.ShapeDtypeStruct(sx4.shape, F32)] * 2,
        scratch_shapes=[
            pltpu.VMEM((S5_CHUNK, m, LANES), F32),
            pltpu.VMEM((m, 2 * SLAB_STATE), F32),
            pltpu.VMEM((m, 2 * SLAB_STATE), F32),
            pltpu.VMEM((S5_CHUNK, m, LANES), F32),
            pltpu.VMEM((2, nb, 2 * SLAB_STATE), F32),
        ],
        compiler_params=_cparams(("parallel", "parallel", "arbitrary")),
        name="s5",
    )(sx4, sx4, wsf, wsb, wi, wof, wob, a_f, a_b)
    return yf.reshape(sx.shape), yb.reshape(sx.shape)


def _s5_direction_terms(lam_re, lam_im, log_dt, b_re, b_im, c_re, c_im):
    hi = lax.Precision.HIGHEST
    dt = jnp.exp(log_dt)[:, None]
    mag = jnp.exp(lam_re * dt)
    a_re = mag * jnp.cos(lam_im * dt)
    a_im = mag * jnp.sin(lam_im * dt)
    den = lam_re * lam_re + lam_im * lam_im
    nr = a_re - 1.0
    k_re = (nr * lam_re + a_im * lam_im) / den
    k_im = (a_im * lam_re - nr * lam_im) / den
    bb_re = k_re[..., None] * b_re - k_im[..., None] * b_im
    bb_im = k_re[..., None] * b_im + k_im[..., None] * b_re
    tau = jnp.arange(S5_CHUNK + 1, dtype=F32)[:, None, None]
    pmag = jnp.exp(lam_re * dt * tau)
    p_re = pmag * jnp.cos(lam_im * dt * tau)
    p_im = pmag * jnp.sin(lam_im * dt * tau)
    cp_re = c_re[None] * p_re[:, :, None, :] - c_im[None] * p_im[:, :, None, :]
    cp_im = c_re[None] * p_im[:, :, None, :] + c_im[None] * p_re[:, :, None, :]
    pb_re = p_re[..., None] * bb_re[None] - p_im[..., None] * bb_im[None]
    pb_im = p_re[..., None] * bb_im[None] + p_im[..., None] * bb_re[None]
    kern = (jnp.einsum('tgdn,gnc->tgdc', cp_re[:-1], bb_re, precision=hi)
            - jnp.einsum('tgdn,gnc->tgdc', cp_im[:-1], bb_im, precision=hi))
    return p_re, p_im, cp_re, cp_im, pb_re, pb_im, kern


def _slab_blockdiag(w):
    _, r1, r2, c1, c2 = w.shape
    w = w.reshape(SLABS, SLAB_GROUPS, r1, r2, c1, c2)
    eye = jnp.eye(SLAB_GROUPS, dtype=w.dtype)
    res = jnp.einsum('sgabcd,gh->sagbchd', w, eye)
    return res.reshape(SLABS, r1 * SLAB_GROUPS * r2, c1 * SLAB_GROUPS * c2)


def _s5_weights(lam_re, lam_im, log_dt, b_re, b_im, c_re, c_im, d):
    tc = S5_CHUNK
    terms = [_s5_direction_terms(lam_re[k], lam_im[k], log_dt[k], b_re[k], b_im[k], c_re[k], c_im[k])
             for k in range(2)]
    out = []
    for k in range(2):
        p_re, p_im, cp_re, cp_im, pb_re, pb_im, kern = terms[k]
        sel = (tc - 1 - np.arange(tc)) if k == 0 else np.arange(tc)
        w = jnp.stack([pb_re[sel], pb_im[sel]], axis=0)
        w = jnp.transpose(w, (2, 1, 4, 0, 3))
        out.append(_slab_blockdiag(w))
    jj = np.arange(tc)
    lag = jj[None, :] - jj[:, None]
    kf = terms[0][6][np.clip(lag, 0, tc - 1)] * jnp.asarray(lag >= 0, F32)[:, :, None, None, None]
    kb = terms[1][6][np.clip(-lag, 0, tc - 1)] * jnp.asarray(lag <= 0, F32)[:, :, None, None, None]
    skip = (jnp.eye(tc, dtype=F32)[:, :, None, None, None]
            * (jnp.eye(SSM_GROUP, dtype=F32)[None, None, None] * d.reshape(SSM_GROUPS, 1, SSM_GROUP)[None, None]))
    wi = kf + kb + skip
    wi = jnp.transpose(wi, (2, 0, 4, 1, 3))
    out.append(_slab_blockdiag(wi))
    for k in range(2):
        p_re, p_im, cp_re, cp_im, pb_re, pb_im, kern = terms[k]
        sel = (np.arange(tc) + 1) if k == 0 else (tc - np.arange(tc))
        w = jnp.stack([cp_re[sel], -cp_im[sel]], axis=0)
        w = jnp.transpose(w, (2, 0, 4, 1, 3))
        out.append(_slab_blockdiag(w))
    mats = [m.astype(BF16) for m in out]
    for k in range(2):
        p_re, p_im = terms[k][0], terms[k][1]
        a = jnp.stack([p_re[tc], p_im[tc]], axis=0).reshape(2, SLABS, SLAB_STATE)
        mats.append(jnp.transpose(a, (1, 0, 2)).reshape(SLABS, 1, 2 * SLAB_STATE))
    return tuple(mats)


def _rel_bucket(rel):
    half = NUM_BUCKETS // 2
    max_exact = half // 2
    ret = (rel > 0).astype(jnp.int32) * half
    n = jnp.abs(rel)
    nf = jnp.maximum(n, 1).astype(jnp.float32)
    large = max_exact + (jnp.log(nf / max_exact) / math.log(MAX_DISTANCE / max_exact)
                         * (half - max_exact)).astype(jnp.int32)
    large = jnp.minimum(large, half - 1)
    return ret + jnp.where(n < max_exact, n, large)


def _bias_tiles(rel_bias, tile):
    assert tile >= MAX_DISTANCE
    qi = jnp.arange(tile, dtype=jnp.int32)[:, None]
    kj = jnp.arange(tile, dtype=jnp.int32)[None, :]
    offs = (jnp.arange(5, dtype=jnp.int32) - 2) * tile
    rel = offs[:, None, None] + kj[None] - qi[None]
    tab = rel_bias[_rel_bucket(rel)].astype(F32)
    return jnp.transpose(tab, (3, 0, 1, 2))


def _attn_kernel(lam_ref, q_ref, k_ref, v_ref, az_ref, bias_ref, g_ref, o_ref,
                 acc_ref, m_ref, l_ref, *, tile, nk):
    qi = pl.program_id(2)
    hd = ATT_HEAD_DIM
    acc_ref[...] = jnp.zeros_like(acc_ref)
    m_ref[...] = jnp.full_like(m_ref, -1e30)
    l_ref[...] = jnp.zeros_like(l_ref)
    q = q_ref[...]
    nt = (((1,), (1,)), ((), ()))

    def body(kj, carry):
        ks = pl.multiple_of(kj * tile, tile)
        kt = k_ref[pl.ds(ks, tile), :]
        vt = v_ref[pl.ds(ks, tile), :]
        bias = bias_ref[jnp.clip(kj - qi, -2, 2) + 2]
        for c in range(2):
            s = lax.dot_general(q[:, c * hd:(c + 1) * hd], kt[:, c * hd:(c + 1) * hd], nt,
                                preferred_element_type=F32) + bias
            m_old = m_ref[c]
            m_new = jnp.maximum(m_old, jnp.max(s, axis=-1, keepdims=True))
            alpha = jnp.exp(m_old - m_new)
            p = jnp.exp(s - m_new)
            l_ref[c] = alpha * l_ref[c] + jnp.sum(p, axis=-1, keepdims=True)
            acc_ref[c] = alpha * acc_ref[c] + _dot(p.astype(BF16), vt)
            m_ref[c] = m_new
        return carry

    lax.fori_loop(0, nk, body, 0)
    lam = lam_ref[0]
    o = acc_ref[0] * (1.0 / l_ref[0]) - acc_ref[1] * (lam / l_ref[1])
    ms = jnp.mean(o * o, axis=-1, keepdims=True)
    o = o * lax.rsqrt(ms + EPS) * g_ref[...] * (1.0 - LAM_INIT)
    az = az_ref[...].astype(F32)
    o_ref[...] = (o * (az * jax.nn.sigmoid(az))).astype(BF16)


def _attention(proj3, bias_tiles, subln_g, lam, tile):
    bt, l, _ = proj3.shape
    nk = l // tile
    blk = ATT_V_DIM
    col = lambda c0: (lambda b, h, i: (b, i, c0 // blk + h))
    kv = lambda c0: (lambda b, h, i: (b, 0, c0 // blk + h))
    return pl.pallas_call(
        functools.partial(_attn_kernel, tile=tile, nk=nk),
        grid=(bt, ATT_HEADS, nk),
        in_specs=[
            pl.BlockSpec(memory_space=pltpu.SMEM),
            pl.BlockSpec((None, tile, blk), col(COL_Q)),
            pl.BlockSpec((None, l, blk), kv(COL_K)),
            pl.BlockSpec((None, l, blk), kv(COL_V)),
            pl.BlockSpec((None, tile, blk), col(COL_AZ)),
            pl.BlockSpec((None, 5, tile, tile), lambda b, h, i: (h, 0, 0, 0)),
            pl.BlockSpec((1, blk), lambda b, h, i: (0, 0)),
        ],
        out_specs=pl.BlockSpec((None, tile, blk), lambda b, h, i: (b, i, h)),
        out_shape=jax.ShapeDtypeStruct((bt, l, ATT_WIDTH), BF16),
        scratch_shapes=[
            pltpu.VMEM((2, tile, blk), F32),
            pltpu.VMEM((2, tile, 1), F32),
            pltpu.VMEM((2, tile, 1), F32),
        ],
        compiler_params=_cparams(("parallel", "parallel", "arbitrary")),
        name="attention",
    )(lam, proj3, proj3, proj3, proj3, bias_tiles, subln_g)


def _branches_kernel(yf_ref, yb_ref, sz_ref, og_ref, gates_ref, gw_ref, gb_ref, ws_ref, wa_ref, o_ref):
    y = jnp.concatenate([yf_ref[s] + yb_ref[s] for s in range(SLABS)], axis=-1)
    y = jax.nn.gelu(y)
    y = y * jax.nn.sigmoid(_dot(y.astype(BF16), gw_ref[...]) + gb_ref[...])
    sz = sz_ref[...].astype(F32)
    y = (y * (sz * jax.nn.sigmoid(sz))).astype(BF16)
    y_s = _dot(y, ws_ref[...])
    y_a = _dot(og_ref[...], wa_ref[...])
    g_s = gates_ref[:, :D_MODEL].astype(F32)
    g_a = gates_ref[:, D_MODEL:].astype(F32)
    o_ref[...] = (jax.nn.sigmoid(g_s) * y_s + jax.nn.sigmoid(g_a) * y_a).astype(BF16)


def _branches(yf, yb, proj, og, glu_w, glu_b, w_s, w_a, tm):
    t = proj.shape[0]
    slab = pl.BlockSpec((SLABS, tm, LANES), lambda i: (0, i, 0))
    return pl.pallas_call(
        _branches_kernel,
        grid=(t // tm,),
        in_specs=[
            slab, slab,
            pl.BlockSpec((tm, SSM_WIDTH), lambda i: (i, COL_SZ // SSM_WIDTH)),
            pl.BlockSpec((tm, ATT_WIDTH), lambda i: (i, 0)),
            pl.BlockSpec((tm, 2 * D_MODEL), lambda i: (i, COL_GATES // (2 * D_MODEL))),
            _const_spec((SSM_WIDTH, SSM_WIDTH)),
            _const_spec((1, SSM_WIDTH)),
            _const_spec((SSM_WIDTH, D_MODEL)),
            _const_spec((ATT_WIDTH, D_MODEL)),
        ],
        out_specs=pl.BlockSpec((tm, D_MODEL), lambda i: (i, 0)),
        out_shape=jax.ShapeDtypeStruct((t, D_MODEL), BF16),
        compiler_params=_cparams(("parallel",)),
        name="branches",
    )(yf, yb, proj, og, proj, glu_w, glu_b, w_s, w_a)


def _rms(x, g):
    ms = jnp.mean(x * x, axis=-1, keepdims=True)
    return x * lax.rsqrt(ms + EPS) * g


def _tail_kernel(m_ref, x_ref, p_ref, wo_ref, png_ref, pgw_ref, ppw_ref, fg_ref, o_ref):
    h = x_ref[...] + _dot(m_ref[...], wo_ref[...])
    gate = jax.nn.sigmoid(_dot(_rms(h, png_ref[...]).astype(BF16), pgw_ref[...]))
    h = h + gate * _dot(p_ref[...].astype(BF16), ppw_ref[...])
    o_ref[...] = _rms(h, fg_ref[...])


def _tail(merged, x2d, p2d, w_out, ple_norm_g, ple_gate_w, ple_proj_w, final_g, tm):
    t = x2d.shape[0]
    row = lambda w: pl.BlockSpec((tm, w), lambda i: (i, 0))
    return pl.pallas_call(
        _tail_kernel,
        grid=(t // tm,),
        in_specs=[
            row(D_MODEL), row(D_MODEL), row(PLE_DIM),
            _const_spec((D_MODEL, D_MODEL)),
            _const_spec((1, D_MODEL)),
            _const_spec((D_MODEL, D_MODEL)),
            _const_spec((PLE_DIM, D_MODEL)),
            _const_spec((1, D_MODEL)),
        ],
        out_specs=row(D_MODEL),
        out_shape=jax.ShapeDtypeStruct((t, D_MODEL), F32),
        compiler_params=_cparams(("parallel",)),
        name="tail",
    )(merged, x2d, p2d, w_out, ple_norm_g, ple_gate_w, ple_proj_w, final_g)


def _prep_w_in(w):
    c = np.cumsum([0, SSM_WIDTH, SSM_WIDTH, ATT_QK_WIDTH, ATT_QK_WIDTH, ATT_WIDTH, ATT_WIDTH, D_MODEL, D_MODEL])
    part = lambda k: w[:, c[k]:c[k + 1]]
    return jnp.concatenate([part(6), part(7), part(1), part(5), part(2), part(3), part(4), part(0)],
                           axis=1).astype(BF16)


def _tiles(bt, l):
    t = bt * l
    tm_in = 1024 if t % 1024 == 0 else l
    tm = 512 if t % 512 == 0 else l
    nb = 8 if bt % 8 == 0 else (4 if bt % 4 == 0 else (2 if bt % 2 == 0 else 1))
    att = 256 if l % 256 == 0 else MAX_DISTANCE
    return tm_in, tm, nb, att


def _trunk(x, p, w):
    bt, l, _ = x.shape
    t = bt * l
    tm_in, tm, nb, att_tile = _tiles(bt, l)
    x2d = x.reshape(t, D_MODEL)
    proj, sx = _inproj(x2d, w['norm_g'], w['w_in'], tm_in)
    yf, yb = _s5(sx, w['s5'], bt, l, nb)
    og = _attention(proj.reshape(bt, l, PROJ_W), _bias_tiles(w['rel_bias'], att_tile), w['subln_g'],
                    w['lam'], att_tile)
    merged = _branches(yf, yb, proj, og.reshape(t, ATT_WIDTH), w['glu_w'], w['glu_b'],
                       w['w_branch_s'], w['w_branch_a'], tm)
    out = _tail(merged, x2d, p.reshape(t, PLE_DIM), w['w_out'], w['ple_norm_g'], w['ple_gate_w'],
                w['ple_proj_w'], w['final_g'], tm)
    return out.reshape(bt, l, D_MODEL)


def _prep_weights(rel_bias, norm_g, w_in, ssm_lambda_re, ssm_lambda_im, ssm_log_dt, ssm_b_re, ssm_b_im,
                  ssm_c_re, ssm_c_im, ssm_d, glu_w, glu_b, lam_q1, lam_k1, lam_q2, lam_k2, subln_g,
                  w_branch_s, w_branch_a, w_out, ple_norm_g, ple_gate_w, ple_proj_w, final_g):
    lam = (jnp.exp(jnp.sum(lam_q1[0] * lam_k1[0])) - jnp.exp(jnp.sum(lam_q2[0] * lam_k2[0])) + LAM_INIT)
    return {
        'rel_bias': rel_bias,
        'norm_g': norm_g[0].reshape(1, D_MODEL),
        'w_in': _prep_w_in(w_in[0]),
        's5': _s5_weights(ssm_lambda_re[0], ssm_lambda_im[0], ssm_log_dt[0], ssm_b_re[0], ssm_b_im[0],
                          ssm_c_re[0], ssm_c_im[0], ssm_d[0]),
        'glu_w': glu_w[0].astype(BF16),
        'glu_b': glu_b[0].reshape(1, SSM_WIDTH),
        'lam': lam.reshape(1).astype(F32),
        'subln_g': subln_g[0].reshape(1, ATT_V_DIM),
        'w_branch_s': w_branch_s[0].astype(BF16),
        'w_branch_a': w_branch_a[0].astype(BF16),
        'w_out': w_out[0].astype(BF16),
        'ple_norm_g': ple_norm_g[0].reshape(1, D_MODEL),
        'ple_gate_w': ple_gate_w[0].astype(BF16),
        'ple_proj_w': ple_proj_w[0].astype(BF16),
        'final_g': final_g.reshape(1, D_MODEL),
    }


def kernel(x_prompt, x_sample, p_prompt, p_sample, rel_bias, norm_g, w_in, ssm_lambda_re, ssm_lambda_im,
           ssm_log_dt, ssm_b_re, ssm_b_im, ssm_c_re, ssm_c_im, ssm_d, glu_w, glu_b, lam_q1, lam_k1, lam_q2,
           lam_k2, subln_g, w_branch_s, w_branch_a, w_out, ple_norm_g, ple_gate_w, ple_proj_w, final_g):
    w = _prep_weights(rel_bias, norm_g, w_in, ssm_lambda_re, ssm_lambda_im, ssm_log_dt, ssm_b_re, ssm_b_im,
                      ssm_c_re, ssm_c_im, ssm_d, glu_w, glu_b, lam_q1, lam_k1, lam_q2, lam_k2, subln_g,
                      w_branch_s, w_branch_a, w_out, ple_norm_g, ple_gate_w, ple_proj_w, final_g)
    return (_trunk(x_prompt, p_prompt[0], w), _trunk(x_sample, p_sample[0], w))
```

```python
import functools
import math

import jax
import jax.numpy as jnp
import numpy as np
from jax import lax
from jax.experimental import pallas as pl
from jax.experimental.pallas import tpu as pltpu

F32 = jnp.float32
BF16 = jnp.bfloat16

D_MODEL = 2048
SSM_WIDTH = 1024
SSM_GROUP = 16
SSM_GROUPS = SSM_WIDTH // SSM_GROUP
SSM_STATE = 64
ATT_HEADS = 4
ATT_HEAD_DIM = 128
ATT_V_DIM = 2 * ATT_HEAD_DIM
ATT_QK_WIDTH = ATT_HEADS * 2 * ATT_HEAD_DIM
ATT_WIDTH = ATT_HEADS * ATT_V_DIM
NUM_BUCKETS = 32
MAX_DISTANCE = 128
PLE_DIM = 256
EPS = 1e-6
LAM_INIT = 0.8 - 0.6 * math.exp(-0.3 * 0)
LOG2E = math.log2(math.e)
ATT_SCALE = ATT_HEAD_DIM ** -0.5 * LOG2E

LANES = 128
SUBLANES = 8
SLABS = SSM_WIDTH // LANES
SLAB_GROUPS = LANES // SSM_GROUP
S5_CHUNK = 8
SLAB_STATE = SLAB_GROUPS * SSM_STATE
S5_K = S5_CHUNK * LANES
S5_ROWS = 512

IN_WIDTH = 2 * SSM_WIDTH + 2 * ATT_QK_WIDTH + 2 * ATT_WIDTH + 2 * D_MODEL
PROJ_W = IN_WIDTH - SSM_WIDTH
COL_GATES = 0
COL_SZ = 4096
COL_AZ = 5120
COL_Q = 6144
COL_K = 7168
COL_V = 8192
IN_TN = 1024
Q_TILE_INDEX = COL_Q // IN_TN
VMEM_LIMIT = 56 * 1024 * 1024


def _cparams(sem):
    return pltpu.CompilerParams(dimension_semantics=sem, vmem_limit_bytes=VMEM_LIMIT)


def _dot(a, b):
    return jnp.dot(a, b, preferred_element_type=F32)


def _const_spec(shape):
    zeros = (0,) * len(shape)
    return pl.BlockSpec(shape, lambda *_: zeros, pipeline_mode=pl.Buffered(1))


def _inproj_kernel(x_ref, g_ref, w_ref, proj_ref, sx_ref, hn_ref):
    j = pl.program_id(1)
    last = pl.num_programs(1) - 1

    @pl.when(j == 0)
    def _():
        x = x_ref[...]
        ms = jnp.mean(x * x, axis=-1, keepdims=True)
        hn_ref[...] = (x * lax.rsqrt(ms + EPS) * g_ref[...]).astype(BF16)

    acc = _dot(hn_ref[...], w_ref[...])

    @pl.when(j < last)
    def _():
        scale = jnp.where(j == Q_TILE_INDEX, ATT_SCALE, 1.0).astype(F32)
        proj_ref[...] = (acc * scale).astype(BF16)

    @pl.when(j == last)
    def _():
        for s in range(SLABS):
            sx_ref[s] = acc[:, s * LANES:(s + 1) * LANES]


def _inproj(x2d, norm_g, w_in, tm):
    t = x2d.shape[0]
    nj = w_in.shape[1] // IN_TN
    return pl.pallas_call(
        _inproj_kernel,
        grid=(t // tm, nj),
        in_specs=[
            pl.BlockSpec((tm, D_MODEL), lambda i, j: (i, 0)),
            pl.BlockSpec((1, D_MODEL), lambda i, j: (0, 0)),
            pl.BlockSpec((D_MODEL, IN_TN), lambda i, j: (0, j)),
        ],
        out_specs=[
            pl.BlockSpec((tm, IN_TN), lambda i, j: (i, jnp.minimum(j, nj - 2))),
            pl.BlockSpec((SLABS, tm, LANES), lambda i, j: (0, i, 0)),
        ],
        out_shape=[
            jax.ShapeDtypeStruct((t, PROJ_W), BF16),
            jax.ShapeDtypeStruct((SLABS, t, LANES), F32),
        ],
        scratch_shapes=[pltpu.VMEM((tm, D_MODEL), BF16)],
        compiler_params=_cparams(("parallel", "arbitrary")),
        name="inproj",
    )(x2d, norm_g, w_in)


def _s5_kernel(xf_ref, xb_ref, wsf_ref, wsb_ref, wi_ref, wof_ref, wob_ref, af_ref, ab_ref,
               yf_ref, yb_ref, xs_ref, sf_ref, sb_ref, ys_ref, carry_ref, *, nb, rb):
    i = pl.program_id(2)
    half = SLAB_STATE

    @pl.when(i == 0)
    def _():
        carry_ref[...] = jnp.zeros_like(carry_ref)

    def gather_rows(x_ref):
        for j in range(S5_CHUNK):
            for b in range(nb):
                xs_ref[j, pl.ds(b, rb, stride=nb), :] = x_ref[b, pl.ds(j, rb, stride=S5_CHUNK), :]
        return jnp.concatenate([xs_ref[j] for j in range(S5_CHUNK)], axis=-1).astype(BF16)

    def scatter_rows(y, y_ref):
        for j in range(S5_CHUNK):
            ys_ref[j] = y[:, j * LANES:(j + 1) * LANES]
        for j in range(S5_CHUNK):
            for b in range(nb):
                y_ref[b, pl.ds(j, rb, stride=S5_CHUNK), :] = ys_ref[j, pl.ds(b, rb, stride=nb), :]

    def scan(s_ref, a_ref, slot, reverse):
        a_re = jnp.broadcast_to(a_ref[:, :half], (nb, half))
        a_im = jnp.broadcast_to(a_ref[:, half:], (nb, half))

        per = max(1, SUBLANES // nb)
        rows = per * nb
        steps = rb // per

        def body(step, carry):
            e_re, e_im = carry
            k = (steps - 1 - step) if reverse else step
            row = pl.multiple_of(k * rows, rows)
            s = s_ref[pl.ds(row, rows), :]
            entering = [None] * per
            for q in (range(per - 1, -1, -1) if reverse else range(per)):
                sq = s[q * nb:(q + 1) * nb]
                entering[q] = jnp.concatenate([e_re, e_im], axis=-1)
                e_re, e_im = (a_re * e_re - a_im * e_im + sq[:, :half],
                              a_re * e_im + a_im * e_re + sq[:, half:])
            s_ref[pl.ds(row, rows), :] = jnp.concatenate(entering, axis=0)
            return e_re, e_im

        c0 = carry_ref[slot]
        e_re, e_im = lax.fori_loop(0, steps, body, (c0[:, :half], c0[:, half:]), unroll=4)
        carry_ref[slot] = jnp.concatenate([e_re, e_im], axis=-1)

    x_f = gather_rows(xf_ref)
    sf_ref[...] = _dot(x_f, wsf_ref[...])
    x_b = gather_rows(xb_ref)
    sb_ref[...] = _dot(x_b, wsb_ref[...])
    scan(sf_ref, af_ref, 0, False)
    scan(sb_ref, ab_ref, 1, True)
    y_f = _dot(x_f, wi_ref[...]) + _dot(sf_ref[...].astype(BF16), wof_ref[...])
    scatter_rows(y_f, yf_ref)
    y_b = _dot(sb_ref[...].astype(BF16), wob_ref[...])
    scatter_rows(y_b, yb_ref)


def _s5(sx, s5w, bt, l, nb):
    lb = min(S5_ROWS * S5_CHUNK // nb, l)
    rb = lb // S5_CHUNK
    nl = l // lb
    sx4 = sx.reshape(SLABS, bt, l, LANES)
    wsf, wsb, wi, wof, wob, a_f, a_b = s5w
    wspec = pl.BlockSpec((None, S5_K, S5_K), lambda s, g, i: (s, 0, 0))
    aspec = pl.BlockSpec((None, 1, 2 * SLAB_STATE), lambda s, g, i: (s, 0, 0))
    fwd = pl.BlockSpec((None, nb, lb, LANES), lambda s, g, i: (s, g, i, 0))
    bwd = pl.BlockSpec((None, nb, lb, LANES), lambda s, g, i: (s, g, nl - 1 - i, 0))
    m = rb * nb
    yf, yb = pl.pallas_call(
        functools.partial(_s5_kernel, nb=nb, rb=rb),
        grid=(SLABS, bt // nb, nl),
        in_specs=[fwd, bwd, wspec, wspec, wspec, wspec, wspec, aspec, aspec],
        out_specs=[fwd, bwd],
        out_shape=[jax.ShapeDtypeStruct(sx4.shape, F32)] * 2,
        scratch_shapes=[
            pltpu.VMEM((S5_CHUNK, m, LANES), F32),
            pltpu.VMEM((m, 2 * SLAB_STATE), F32),
            pltpu.VMEM((m, 2 * SLAB_STATE), F32),
            pltpu.VMEM((S5_CHUNK, m, LANES), F32),
            pltpu.VMEM((2, nb, 2 * SLAB_STATE), F32),
        ],
        compiler_params=_cparams(("parallel", "parallel", "arbitrary")),
        name="s5",
    )(sx4, sx4, wsf, wsb, wi, wof, wob, a_f, a_b)
    return yf.reshape(sx.shape), yb.reshape(sx.shape)


def _s5_direction_terms(lam_re, lam_im, log_dt, b_re, b_im, c_re, c_im):
    hi = lax.Precision.HIGHEST
    dt = jnp.exp(log_dt)[:, None]
    mag = jnp.exp(lam_re * dt)
    a_re = mag * jnp.cos(lam_im * dt)
    a_im = mag * jnp.sin(lam_im * dt)
    den = lam_re * lam_re + lam_im * lam_im
    nr = a_re - 1.0
    k_re = (nr * lam_re + a_im * lam_im) / den
    k_im = (a_im * lam_re - nr * lam_im) / den
    bb_re = k_re[..., None] * b_re - k_im[..., None] * b_im
    bb_im = k_re[..., None] * b_im + k_im[..., None] * b_re
    tau = jnp.arange(S5_CHUNK + 1, dtype=F32)[:, None, None]
    pmag = jnp.exp(lam_re * dt * tau)
    p_re = pmag * jnp.cos(lam_im * dt * tau)
    p_im = pmag * jnp.sin(lam_im * dt * tau)
    cp_re = c_re[None] * p_re[:, :, None, :] - c_im[None] * p_im[:, :, None, :]
    cp_im = c_re[None] * p_im[:, :, None, :] + c_im[None] * p_re[:, :, None, :]
    pb_re = p_re[..., None] * bb_re[None] - p_im[..., None] * bb_im[None]
    pb_im = p_re[..., None] * bb_im[None] + p_im[..., None] * bb_re[None]
    kern = (jnp.einsum('tgdn,gnc->tgdc', cp_re[:-1], bb_re, precision=hi)
            - jnp.einsum('tgdn,gnc->tgdc', cp_im[:-1], bb_im, precision=hi))
    return p_re, p_im, cp_re, cp_im, pb_re, pb_im, kern


def _slab_blockdiag(w):
    _, r1, r2, c1, c2 = w.shape
    w = w.reshape(SLABS, SLAB_GROUPS, r1, r2, c1, c2)
    eye = jnp.eye(SLAB_GROUPS, dtype=w.dtype)
    res = jnp.einsum('sgabcd,gh->sagbchd', w, eye)
    return res.reshape(SLABS, r1 * SLAB_GROUPS * r2, c1 * SLAB_GROUPS * c2)


def _s5_weights(lam_re, lam_im, log_dt, b_re, b_im, c_re, c_im, d):
    tc = S5_CHUNK
    terms = [_s5_direction_terms(lam_re[k], lam_im[k], log_dt[k], b_re[k], b_im[k], c_re[k], c_im[k])
             for k in range(2)]
    out = []
    for k in range(2):
        p_re, p_im, cp_re, cp_im, pb_re, pb_im, kern = terms[k]
        sel = (tc - 1 - np.arange(tc)) if k == 0 else np.arange(tc)
        w = jnp.stack([pb_re[sel], pb_im[sel]], axis=0)
        w = jnp.transpose(w, (2, 1, 4, 0, 3))
        out.append(_slab_blockdiag(w))
    jj = np.arange(tc)
    lag = jj[None, :] - jj[:, None]
    kf = terms[0][6][np.clip(lag, 0, tc - 1)] * jnp.asarray(lag >= 0, F32)[:, :, None, None, None]
    kb = terms[1][6][np.clip(-lag, 0, tc - 1)] * jnp.asarray(lag <= 0, F32)[:, :, None, None, None]
    skip = (jnp.eye(tc, dtype=F32)[:, :, None, None, None]
            * (jnp.eye(SSM_GROUP, dtype=F32)[None, None, None] * d.reshape(SSM_GROUPS, 1, SSM_GROUP)[None, None]))
    wi = kf + kb + skip
    wi = jnp.transpose(wi, (2, 0, 4, 1, 3))
    out.append(_slab_blockdiag(wi))
    for k in range(2):
        p_re, p_im, cp_re, cp_im, pb_re, pb_im, kern = terms[k]
        sel = (np.arange(tc) + 1) if k == 0 else (tc - np.arange(tc))
        w = jnp.stack([cp_re[sel], -cp_im[sel]], axis=0)
        w = jnp.transpose(w, (2, 0, 4, 1, 3))
        out.append(_slab_blockdiag(w))
    mats = [m.astype(BF16) for m in out]
    for k in range(2):
        p_re, p_im = terms[k][0], terms[k][1]
        a = jnp.stack([p_re[tc], p_im[tc]], axis=0).reshape(2, SLABS, SLAB_STATE)
        mats.append(jnp.transpose(a, (1, 0, 2)).reshape(SLABS, 1, 2 * SLAB_STATE))
    return tuple(mats)


def _rel_bucket(rel):
    half = NUM_BUCKETS // 2
    max_exact = half // 2
    ret = (rel > 0).astype(jnp.int32) * half
    n = jnp.abs(rel)
    nf = jnp.maximum(n, 1).astype(jnp.float32)
    large = max_exact + (jnp.log(nf / max_exact) / math.log(MAX_DISTANCE / max_exact)
                         * (half - max_exact)).astype(jnp.int32)
    large = jnp.minimum(large, half - 1)
    return ret + jnp.where(n < max_exact, n, large)


def _bias_tiles(rel_bias, tile):
    assert tile >= MAX_DISTANCE
    qi = jnp.arange(tile, dtype=jnp.int32)[:, None]
    kj = jnp.arange(tile, dtype=jnp.int32)[None, :]
    offs = (jnp.arange(5, dtype=jnp.int32) - 2) * tile
    rel = offs[:, None, None] + kj[None] - qi[None]
    onehot = (_rel_bucket(rel)[None] == jnp.arange(NUM_BUCKETS, dtype=jnp.int32)[:, None, None, None])
    tab = jnp.einsum('nh,nabc->habc', rel_bias.astype(F32) * LOG2E, onehot.astype(F32),
                     precision=lax.Precision.HIGHEST)
    return tab


def _attn_kernel(lam_ref, q_ref, k_ref, v_ref, az_ref, bias_ref, g_ref, o_ref, s_ref, p_ref, *, tile, nk):
    qi = pl.program_id(2)
    hd = ATT_HEAD_DIM
    nt = (((1,), (1,)), ((), ()))
    lane_blocks = tile // LANES
    bidx = [jnp.clip(kj - qi, -2, 2) + 2 for kj in range(nk)]
    acc, lsum = [], []
    for c in range(2):
        qc = q_ref[:, c * hd:(c + 1) * hd]
        mx = jnp.full((tile, LANES), -jnp.inf, F32)
        for kj in range(nk):
            kc = k_ref[kj * tile:(kj + 1) * tile, c * hd:(c + 1) * hd]
            s = lax.dot_general(qc, kc, nt, preferred_element_type=F32) + bias_ref[bidx[kj]]
            s_ref[c, :, kj * tile:(kj + 1) * tile] = s
            for b in range(lane_blocks):
                mx = jnp.maximum(mx, s[:, b * LANES:(b + 1) * LANES])
        m = jnp.max(mx, axis=1, keepdims=True)
        ls = jnp.zeros((tile, LANES), F32)
        for kj in range(nk):
            p = jnp.exp2(s_ref[c, :, kj * tile:(kj + 1) * tile] - m)
            for b in range(lane_blocks):
                ls = ls + p[:, b * LANES:(b + 1) * LANES]
            p_ref[c, :, kj * tile:(kj + 1) * tile] = p.astype(BF16)
        lsum.append(jnp.sum(ls, axis=1, keepdims=True))
        acc.append(_dot(p_ref[c], v_ref[...]))
    lam = lam_ref[0]
    o = acc[0] * (1.0 / lsum[0]) - acc[1] * (lam / lsum[1])
    ms = jnp.mean(o * o, axis=-1, keepdims=True)
    o = o * lax.rsqrt(ms + EPS) * g_ref[...] * (1.0 - LAM_INIT)
    az = az_ref[...].astype(F32)
    o_ref[...] = (o * (az * jax.nn.sigmoid(az))).astype(BF16)


def _attention(proj3, bias_tiles, subln_g, lam, tile):
    bt, l, _ = proj3.shape
    nk = l // tile
    blk = ATT_V_DIM
    col = lambda c0: (lambda b, h, i: (b, i, c0 // blk + h))
    kv = lambda c0: (lambda b, h, i: (b, 0, c0 // blk + h))
    return pl.pallas_call(
        functools.partial(_attn_kernel, tile=tile, nk=nk),
        grid=(bt, ATT_HEADS, nk),
        in_specs=[
            pl.BlockSpec(memory_space=pltpu.SMEM),
            pl.BlockSpec((None, tile, blk), col(COL_Q)),
            pl.BlockSpec((None, l, blk), kv(COL_K)),
            pl.BlockSpec((None, l, blk), kv(COL_V)),
            pl.BlockSpec((None, tile, blk), col(COL_AZ)),
            pl.BlockSpec((None, 5, tile, tile), lambda b, h, i: (h, 0, 0, 0)),
            pl.BlockSpec((1, blk), lambda b, h, i: (0, 0)),
        ],
        out_specs=pl.BlockSpec((None, tile, blk), lambda b, h, i: (b, i, h)),
        out_shape=jax.ShapeDtypeStruct((bt, l, ATT_WIDTH), BF16),
        scratch_shapes=[
            pltpu.VMEM((2, tile, l), F32),
            pltpu.VMEM((2, tile, l), BF16),
        ],
        compiler_params=_cparams(("parallel", "parallel", "arbitrary")),
        name="attention",
    )(lam, proj3, proj3, proj3, proj3, bias_tiles, subln_g)


def _branches_kernel(yf_ref, yb_ref, sz_ref, og_ref, gates_ref, gw_ref, gb_ref, ws_ref, wa_ref, o_ref):
    y = jnp.concatenate([yf_ref[s] + yb_ref[s] for s in range(SLABS)], axis=-1)
    y = jax.nn.gelu(y)
    y = y * jax.nn.sigmoid(_dot(y.astype(BF16), gw_ref[...]) + gb_ref[...])
    sz = sz_ref[...].astype(F32)
    y = (y * (sz * jax.nn.sigmoid(sz))).astype(BF16)
    y_s = _dot(y, ws_ref[...])
    y_a = _dot(og_ref[...], wa_ref[...])
    g_s = gates_ref[:, :D_MODEL].astype(F32)
    g_a = gates_ref[:, D_MODEL:].astype(F32)
    o_ref[...] = (jax.nn.sigmoid(g_s) * y_s + jax.nn.sigmoid(g_a) * y_a).astype(BF16)


def _branches(yf, yb, proj, og, glu_w, glu_b, w_s, w_a, tm):
    t = proj.shape[0]
    slab = pl.BlockSpec((SLABS, tm, LANES), lambda i: (0, i, 0))
    return pl.pallas_call(
        _branches_kernel,
        grid=(t // tm,),
        in_specs=[
            slab, slab,
            pl.BlockSpec((tm, SSM_WIDTH), lambda i: (i, COL_SZ // SSM_WIDTH)),
            pl.BlockSpec((tm, ATT_WIDTH), lambda i: (i, 0)),
            pl.BlockSpec((tm, 2 * D_MODEL), lambda i: (i, COL_GATES // (2 * D_MODEL))),
            _const_spec((SSM_WIDTH, SSM_WIDTH)),
            _const_spec((1, SSM_WIDTH)),
            _const_spec((SSM_WIDTH, D_MODEL)),
            _const_spec((ATT_WIDTH, D_MODEL)),
        ],
        out_specs=pl.BlockSpec((tm, D_MODEL), lambda i: (i, 0)),
        out_shape=jax.ShapeDtypeStruct((t, D_MODEL), BF16),
        compiler_params=_cparams(("parallel",)),
        name="branches",
    )(yf, yb, proj, og, proj, glu_w, glu_b, w_s, w_a)


def _rms(x, g):
    ms = jnp.mean(x * x, axis=-1, keepdims=True)
    return x * lax.rsqrt(ms + EPS) * g


def _tail_kernel(m_ref, x_ref, p_ref, wo_ref, png_ref, pgw_ref, ppw_ref, fg_ref, o_ref):
    h = x_ref[...] + _dot(m_ref[...], wo_ref[...])
    gate = jax.nn.sigmoid(_dot(_rms(h, png_ref[...]).astype(BF16), pgw_ref[...]))
    h = h + gate * _dot(p_ref[...].astype(BF16), ppw_ref[...])
    o_ref[...] = _rms(h, fg_ref[...])


def _tail(merged, x2d, p2d, w_out, ple_norm_g, ple_gate_w, ple_proj_w, final_g, tm):
    t = x2d.shape[0]
    row = lambda w: pl.BlockSpec((tm, w), lambda i: (i, 0))
    return pl.pallas_call(
        _tail_kernel,
        grid=(t // tm,),
        in_specs=[
            row(D_MODEL), row(D_MODEL), row(PLE_DIM),
            _const_spec((D_MODEL, D_MODEL)),
            _const_spec((1, D_MODEL)),
            _const_spec((D_MODEL, D_MODEL)),
            _const_spec((PLE_DIM, D_MODEL)),
            _const_spec((1, D_MODEL)),
        ],
        out_specs=row(D_MODEL),
        out_shape=jax.ShapeDtypeStruct((t, D_MODEL), F32),
        compiler_params=_cparams(("parallel",)),
        name="tail",
    )(merged, x2d, p2d, w_out, ple_norm_g, ple_gate_w, ple_proj_w, final_g)


def _prep_w_in(w):
    c = np.cumsum([0, SSM_WIDTH, SSM_WIDTH, ATT_QK_WIDTH, ATT_QK_WIDTH, ATT_WIDTH, ATT_WIDTH, D_MODEL, D_MODEL])
    part = lambda k: w[:, c[k]:c[k + 1]]
    return jnp.concatenate([part(6), part(7), part(1), part(5), part(2), part(3), part(4), part(0)],
                           axis=1).astype(BF16)


def _tiles(bt, l):
    t = bt * l
    tm_in = 1024 if t % 1024 == 0 else l
    tm = 512 if t % 512 == 0 else l
    nb = 8 if bt % 8 == 0 else (4 if bt % 4 == 0 else (2 if bt % 2 == 0 else 1))
    att = 256 if l % 256 == 0 else MAX_DISTANCE
    return tm_in, tm, nb, att


def _trunk(x, p, w):
    bt, l, _ = x.shape
    t = bt * l
    tm_in, tm, nb, att_tile = _tiles(bt, l)
    x2d = x.reshape(t, D_MODEL)
    proj, sx = _inproj(x2d, w['norm_g'], w['w_in'], tm_in)
    yf, yb = _s5(sx, w['s5'], bt, l, nb)
    og = _attention(proj.reshape(bt, l, PROJ_W), _bias_tiles(w['rel_bias'], att_tile), w['subln_g'],
                    w['lam'], att_tile)
    merged = _branches(yf, yb, proj, og.reshape(t, ATT_WIDTH), w['glu_w'], w['glu_b'],
                       w['w_branch_s'], w['w_branch_a'], tm)
    out = _tail(merged, x2d, p.reshape(t, PLE_DIM), w['w_out'], w['ple_norm_g'], w['ple_gate_w'],
                w['ple_proj_w'], w['final_g'], tm)
    return out.reshape(bt, l, D_MODEL)


def _prep_weights(rel_bias, norm_g, w_in, ssm_lambda_re, ssm_lambda_im, ssm_log_dt, ssm_b_re, ssm_b_im,
                  ssm_c_re, ssm_c_im, ssm_d, glu_w, glu_b, lam_q1, lam_k1, lam_q2, lam_k2, subln_g,
                  w_branch_s, w_branch_a, w_out, ple_norm_g, ple_gate_w, ple_proj_w, final_g):
    lam = (jnp.exp(jnp.sum(lam_q1[0] * lam_k1[0])) - jnp.exp(jnp.sum(lam_q2[0] * lam_k2[0])) + LAM_INIT)
    return {
        'rel_bias': rel_bias,
        'norm_g': norm_g[0].reshape(1, D_MODEL),
        'w_in': _prep_w_in(w_in[0]),
        's5': _s5_weights(ssm_lambda_re[0], ssm_lambda_im[0], ssm_log_dt[0], ssm_b_re[0], ssm_b_im[0],
                          ssm_c_re[0], ssm_c_im[0], ssm_d[0]),
        'glu_w': glu_w[0].astype(BF16),
        'glu_b': glu_b[0].reshape(1, SSM_WIDTH),
        'lam': lam.reshape(1).astype(F32),
        'subln_g': subln_g[0].reshape(1, ATT_V_DIM),
        'w_branch_s': w_branch_s[0].astype(BF16),
        'w_branch_a': w_branch_a[0].astype(BF16),
        'w_out': w_out[0].astype(BF16),
        'ple_norm_g': ple_norm_g[0].reshape(1, D_MODEL),
        'ple_gate_w': ple_gate_w[0].astype(BF16),
        'ple_proj_w': ple_proj_w[0].astype(BF16),
        'final_g': final_g.reshape(1, D_MODEL),
    }


def kernel(x_prompt, x_sample, p_prompt, p_sample, rel_bias, norm_g, w_in, ssm_lambda_re, ssm_lambda_im,
           ssm_log_dt, ssm_b_re, ssm_b_im, ssm_c_re, ssm_c_im, ssm_d, glu_w, glu_b, lam_q1, lam_k1, lam_q2,
           lam_k2, subln_g, w_branch_s, w_branch_a, w_out, ple_norm_g, ple_gate_w, ple_proj_w, final_g):
    w = _prep_weights(rel_bias, norm_g, w_in, ssm_lambda_re, ssm_lambda_im, ssm_log_dt, ssm_b_re, ssm_b_im,
                      ssm_c_re, ssm_c_im, ssm_d, glu_w, glu_b, lam_q1, lam_k1, lam_q2, lam_k2, subln_g,
                      w_branch_s, w_branch_a, w_out, ple_norm_g, ple_gate_w, ple_proj_w, final_g)
    return (_trunk(x_prompt, p_prompt[0], w), _trunk(x_sample, p_sample[0], w))
```

```python
import functools
import math

import jax
import jax.numpy as jnp
import numpy as np
from jax import lax
from jax.experimental import pallas as pl
from jax.experimental.pallas import tpu as pltpu

F32 = jnp.float32
BF16 = jnp.bfloat16

D_MODEL = 2048
SSM_WIDTH = 1024
SSM_GROUP = 16
SSM_GROUPS = SSM_WIDTH // SSM_GROUP
SSM_STATE = 64
ATT_HEADS = 4
ATT_HEAD_DIM = 128
ATT_V_DIM = 2 * ATT_HEAD_DIM
ATT_QK_WIDTH = ATT_HEADS * 2 * ATT_HEAD_DIM
ATT_WIDTH = ATT_HEADS * ATT_V_DIM
NUM_BUCKETS = 32
MAX_DISTANCE = 128
PLE_DIM = 256
EPS = 1e-6
LAM_INIT = 0.8 - 0.6 * math.exp(-0.3 * 0)
LOG2E = math.log2(math.e)
ATT_SCALE = ATT_HEAD_DIM ** -0.5 * LOG2E

LANES = 128
SUBLANES = 8
SLABS = SSM_WIDTH // LANES
SLAB_GROUPS = LANES // SSM_GROUP
S5_CHUNK = 8
SLAB_STATE = SLAB_GROUPS * SSM_STATE
S5_K = S5_CHUNK * LANES
S5_ROWS = 512

IN_WIDTH = 2 * SSM_WIDTH + 2 * ATT_QK_WIDTH + 2 * ATT_WIDTH + 2 * D_MODEL
PROJ_W = IN_WIDTH - SSM_WIDTH
COL_GATES = 0
COL_SZ = 4096
COL_AZ = 5120
COL_Q = 6144
COL_K = 7168
COL_V = 8192
SOFTMAX_ROWS = 32
IN_TN = 1024
Q_TILE_INDEX = COL_Q // IN_TN
VMEM_LIMIT = 56 * 1024 * 1024


def _cparams(sem):
    return pltpu.CompilerParams(dimension_semantics=sem, vmem_limit_bytes=VMEM_LIMIT)


def _dot(a, b):
    return jnp.dot(a, b, preferred_element_type=F32)


def _const_spec(shape):
    zeros = (0,) * len(shape)
    return pl.BlockSpec(shape, lambda *_: zeros, pipeline_mode=pl.Buffered(1))


def _inproj_kernel(x_ref, g_ref, w_ref, proj_ref, sx_ref, hn_ref):
    j = pl.program_id(1)
    last = pl.num_programs(1) - 1

    @pl.when(j == 0)
    def _():
        x = x_ref[...]
        ms = jnp.mean(x * x, axis=-1, keepdims=True)
        hn_ref[...] = (x * lax.rsqrt(ms + EPS) * g_ref[...]).astype(BF16)

    acc = _dot(hn_ref[...], w_ref[...])

    @pl.when(j < last)
    def _():
        scale = jnp.where(j == Q_TILE_INDEX, ATT_SCALE, 1.0).astype(F32)
        proj_ref[...] = (acc * scale).astype(BF16)

    @pl.when(j == last)
    def _():
        for s in range(SLABS):
            sx_ref[s] = acc[:, s * LANES:(s + 1) * LANES]


def _inproj(x2d, norm_g, w_in, tm):
    t = x2d.shape[0]
    nj = w_in.shape[1] // IN_TN
    return pl.pallas_call(
        _inproj_kernel,
        grid=(t // tm, nj),
        in_specs=[
            pl.BlockSpec((tm, D_MODEL), lambda i, j: (i, 0)),
            pl.BlockSpec((1, D_MODEL), lambda i, j: (0, 0)),
            pl.BlockSpec((D_MODEL, IN_TN), lambda i, j: (0, j)),
        ],
        out_specs=[
            pl.BlockSpec((tm, IN_TN), lambda i, j: (i, jnp.minimum(j, nj - 2))),
            pl.BlockSpec((SLABS, tm, LANES), lambda i, j: (0, i, 0)),
        ],
        out_shape=[
            jax.ShapeDtypeStruct((t, PROJ_W), BF16),
            jax.ShapeDtypeStruct((SLABS, t, LANES), F32),
        ],
        scratch_shapes=[pltpu.VMEM((tm, D_MODEL), BF16)],
        compiler_params=_cparams(("parallel", "arbitrary")),
        name="inproj",
    )(x2d, norm_g, w_in)


def _s5_kernel(xf_ref, xb_ref, wsf_ref, wsb_ref, wi_ref, wof_ref, wob_ref, af_ref, ab_ref,
               yf_ref, yb_ref, xs_ref, sf_ref, sb_ref, ys_ref, carry_ref, *, nb, rb):
    i = pl.program_id(2)
    half = SLAB_STATE

    @pl.when(i == 0)
    def _():
        carry_ref[...] = jnp.zeros_like(carry_ref)

    def gather_rows(x_ref):
        for j in range(S5_CHUNK):
            for b in range(nb):
                xs_ref[j, pl.ds(b, rb, stride=nb), :] = x_ref[b, pl.ds(j, rb, stride=S5_CHUNK), :]
        return jnp.concatenate([xs_ref[j] for j in range(S5_CHUNK)], axis=-1).astype(BF16)

    def scatter_rows(y, y_ref):
        for j in range(S5_CHUNK):
            ys_ref[j] = y[:, j * LANES:(j + 1) * LANES]
        for j in range(S5_CHUNK):
            for b in range(nb):
                y_ref[b, pl.ds(j, rb, stride=S5_CHUNK), :] = ys_ref[j, pl.ds(b, rb, stride=nb), :]

    def scan(s_ref, a_ref, slot, reverse):
        a_re = jnp.broadcast_to(a_ref[:, :half], (nb, half))
        a_im = jnp.broadcast_to(a_ref[:, half:], (nb, half))

        per = max(1, SUBLANES // nb)
        rows = per * nb
        steps = rb // per

        def body(step, carry):
            e_re, e_im = carry
            k = (steps - 1 - step) if reverse else step
            row = pl.multiple_of(k * rows, rows)
            s = s_ref[pl.ds(row, rows), :]
            entering = [None] * per
            for q in (range(per - 1, -1, -1) if reverse else range(per)):
                sq = s[q * nb:(q + 1) * nb]
                entering[q] = jnp.concatenate([e_re, e_im], axis=-1)
                e_re, e_im = (a_re * e_re - a_im * e_im + sq[:, :half],
                              a_re * e_im + a_im * e_re + sq[:, half:])
            s_ref[pl.ds(row, rows), :] = jnp.concatenate(entering, axis=0)
            return e_re, e_im

        c0 = carry_ref[slot]
        e_re, e_im = lax.fori_loop(0, steps, body, (c0[:, :half], c0[:, half:]), unroll=4)
        carry_ref[slot] = jnp.concatenate([e_re, e_im], axis=-1)

    x_f = gather_rows(xf_ref)
    sf_ref[...] = _dot(x_f, wsf_ref[...])
    x_b = gather_rows(xb_ref)
    sb_ref[...] = _dot(x_b, wsb_ref[...])
    scan(sf_ref, af_ref, 0, False)
    scan(sb_ref, ab_ref, 1, True)
    y_f = _dot(x_f, wi_ref[...]) + _dot(sf_ref[...].astype(BF16), wof_ref[...])
    scatter_rows(y_f, yf_ref)
    y_b = _dot(sb_ref[...].astype(BF16), wob_ref[...])
    scatter_rows(y_b, yb_ref)


def _s5(sx, s5w, bt, l, nb):
    lb = min(S5_ROWS * S5_CHUNK // nb, l)
    rb = lb // S5_CHUNK
    nl = l // lb
    sx4 = sx.reshape(SLABS, bt, l, LANES)
    wsf, wsb, wi, wof, wob, a_f, a_b = s5w
    wspec = pl.BlockSpec((None, S5_K, S5_K), lambda s, g, i: (s, 0, 0))
    aspec = pl.BlockSpec((None, 1, 2 * SLAB_STATE), lambda s, g, i: (s, 0, 0))
    fwd = pl.BlockSpec((None, nb, lb, LANES), lambda s, g, i: (s, g, i, 0))
    bwd = pl.BlockSpec((None, nb, lb, LANES), lambda s, g, i: (s, g, nl - 1 - i, 0))
    m = rb * nb
    yf, yb = pl.pallas_call(
        functools.partial(_s5_kernel, nb=nb, rb=rb),
        grid=(SLABS, bt // nb, nl),
        in_specs=[fwd, bwd, wspec, wspec, wspec, wspec, wspec, aspec, aspec],
        out_specs=[fwd, bwd],
        out_shape=[jax.ShapeDtypeStruct(sx4.shape, F32)] * 2,
        scratch_shapes=[
            pltpu.VMEM((S5_CHUNK, m, LANES), F32),
            pltpu.VMEM((m, 2 * SLAB_STATE), F32),
            pltpu.VMEM((m, 2 * SLAB_STATE), F32),
            pltpu.VMEM((S5_CHUNK, m, LANES), F32),
            pltpu.VMEM((2, nb, 2 * SLAB_STATE), F32),
        ],
        compiler_params=_cparams(("parallel", "parallel", "arbitrary")),
        name="s5",
    )(sx4, sx4, wsf, wsb, wi, wof, wob, a_f, a_b)
    return yf.reshape(sx.shape), yb.reshape(sx.shape)


def _s5_direction_terms(lam_re, lam_im, log_dt, b_re, b_im, c_re, c_im):
    hi = lax.Precision.HIGHEST
    dt = jnp.exp(log_dt)[:, None]
    mag = jnp.exp(lam_re * dt)
    a_re = mag * jnp.cos(lam_im * dt)
    a_im = mag * jnp.sin(lam_im * dt)
    den = lam_re * lam_re + lam_im * lam_im
    nr = a_re - 1.0
    k_re = (nr * lam_re + a_im * lam_im) / den
    k_im = (a_im * lam_re - nr * lam_im) / den
    bb_re = k_re[..., None] * b_re - k_im[..., None] * b_im
    bb_im = k_re[..., None] * b_im + k_im[..., None] * b_re
    tau = jnp.arange(S5_CHUNK + 1, dtype=F32)[:, None, None]
    pmag = jnp.exp(lam_re * dt * tau)
    p_re = pmag * jnp.cos(lam_im * dt * tau)
    p_im = pmag * jnp.sin(lam_im * dt * tau)
    cp_re = c_re[None] * p_re[:, :, None, :] - c_im[None] * p_im[:, :, None, :]
    cp_im = c_re[None] * p_im[:, :, None, :] + c_im[None] * p_re[:, :, None, :]
    pb_re = p_re[..., None] * bb_re[None] - p_im[..., None] * bb_im[None]
    pb_im = p_re[..., None] * bb_im[None] + p_im[..., None] * bb_re[None]
    kern = (jnp.einsum('tgdn,gnc->tgdc', cp_re[:-1], bb_re, precision=hi)
            - jnp.einsum('tgdn,gnc->tgdc', cp_im[:-1], bb_im, precision=hi))
    return p_re, p_im, cp_re, cp_im, pb_re, pb_im, kern


def _slab_blockdiag(w):
    _, r1, r2, c1, c2 = w.shape
    nrow, ncol = r1 * SLAB_GROUPS * r2, c1 * SLAB_GROUPS * c2
    w = w.astype(BF16).reshape(SLABS, SLAB_GROUPS, r1, r2, c1 * c2)
    rows = jnp.transpose(w, (0, 2, 1, 3, 4)).reshape(SLABS * nrow, c1 * c2)
    col = jnp.arange(ncol, dtype=jnp.int32)
    src = (col // (SLAB_GROUPS * c2)) * c2 + col % c2
    spread = (jnp.arange(c1 * c2, dtype=jnp.int32)[:, None] == src[None, :]).astype(BF16)
    dense = jnp.dot(rows, spread, preferred_element_type=BF16).reshape(SLABS, nrow, ncol)
    row_g = (jnp.arange(nrow, dtype=jnp.int32) // r2) % SLAB_GROUPS
    col_h = (col // c2) % SLAB_GROUPS
    return jnp.where((row_g[:, None] == col_h[None, :])[None], dense, jnp.zeros((), BF16))


def _s5_weights(lam_re, lam_im, log_dt, b_re, b_im, c_re, c_im, d):
    tc = S5_CHUNK
    terms = [_s5_direction_terms(lam_re[k], lam_im[k], log_dt[k], b_re[k], b_im[k], c_re[k], c_im[k])
             for k in range(2)]
    out = []
    for k in range(2):
        p_re, p_im, cp_re, cp_im, pb_re, pb_im, kern = terms[k]
        sel = (tc - 1 - np.arange(tc)) if k == 0 else np.arange(tc)
        w = jnp.stack([pb_re[sel], pb_im[sel]], axis=0)
        w = jnp.transpose(w, (2, 1, 4, 0, 3))
        out.append(_slab_blockdiag(w))
    jj = np.arange(tc)
    lag = jj[None, :] - jj[:, None]
    kf = terms[0][6][np.clip(lag, 0, tc - 1)] * jnp.asarray(lag >= 0, F32)[:, :, None, None, None]
    kb = terms[1][6][np.clip(-lag, 0, tc - 1)] * jnp.asarray(lag <= 0, F32)[:, :, None, None, None]
    skip = (jnp.eye(tc, dtype=F32)[:, :, None, None, None]
            * (jnp.eye(SSM_GROUP, dtype=F32)[None, None, None] * d.reshape(SSM_GROUPS, 1, SSM_GROUP)[None, None]))
    wi = kf + kb + skip
    wi = jnp.transpose(wi, (2, 0, 4, 1, 3))
    out.append(_slab_blockdiag(wi))
    for k in range(2):
        p_re, p_im, cp_re, cp_im, pb_re, pb_im, kern = terms[k]
        sel = (np.arange(tc) + 1) if k == 0 else (tc - np.arange(tc))
        w = jnp.stack([cp_re[sel], -cp_im[sel]], axis=0)
        w = jnp.transpose(w, (2, 0, 4, 1, 3))
        out.append(_slab_blockdiag(w))
    mats = list(out)
    for k in range(2):
        p_re, p_im = terms[k][0], terms[k][1]
        a = jnp.stack([p_re[tc], p_im[tc]], axis=0).reshape(2, SLABS, SLAB_STATE)
        mats.append(jnp.transpose(a, (1, 0, 2)).reshape(SLABS, 1, 2 * SLAB_STATE))
    return tuple(mats)


def _rel_bucket(rel):
    half = NUM_BUCKETS // 2
    max_exact = half // 2
    ret = (rel > 0).astype(jnp.int32) * half
    n = jnp.abs(rel)
    nf = jnp.maximum(n, 1).astype(jnp.float32)
    large = max_exact + (jnp.log(nf / max_exact) / math.log(MAX_DISTANCE / max_exact)
                         * (half - max_exact)).astype(jnp.int32)
    large = jnp.minimum(large, half - 1)
    return ret + jnp.where(n < max_exact, n, large)


def _bias_tiles(rel_bias, tile):
    assert tile >= MAX_DISTANCE
    qi = jnp.arange(tile, dtype=jnp.int32)[:, None]
    kj = jnp.arange(tile, dtype=jnp.int32)[None, :]
    offs = (jnp.arange(5, dtype=jnp.int32) - 2) * tile
    rel = offs[:, None, None] + kj[None] - qi[None]
    onehot = (_rel_bucket(rel)[None] == jnp.arange(NUM_BUCKETS, dtype=jnp.int32)[:, None, None, None])
    tab = jnp.einsum('nh,nabc->habc', rel_bias.astype(F32) * LOG2E, onehot.astype(F32),
                     precision=lax.Precision.HIGHEST)
    return tab


def _attn_kernel(lam_ref, q_ref, k_ref, v_ref, bias_ref, g_ref, o_ref,
                 sa_ref, sb_ref, pa_ref, pb_ref, la_ref, lb_ref, *, tile, nk):
    hd = ATT_HEAD_DIM
    nt = (((1,), (1,)), ((), ()))
    lane_blocks = tile // LANES
    nq = nk
    assert nq % 2 == 0

    def tile_rows(i):
        return pl.ds(pl.multiple_of(i * tile, tile), tile)

    def scores(i, s_ref, p_ref, l_ref):
        for c in range(2):
            qc = q_ref[tile_rows(i), c * hd:(c + 1) * hd]
            mx = jnp.full((tile, LANES), -jnp.inf, F32)
            for kj in range(nk):
                kc = k_ref[kj * tile:(kj + 1) * tile, c * hd:(c + 1) * hd]
                s = (lax.dot_general(qc, kc, nt, preferred_element_type=F32)
                     + bias_ref[jnp.clip(kj - i, -2, 2) + 2])
                s_ref[c, :, kj * tile:(kj + 1) * tile] = s
                for b in range(lane_blocks):
                    mx = jnp.maximum(mx, s[:, b * LANES:(b + 1) * LANES])
            m = jnp.broadcast_to(jnp.max(mx, axis=1, keepdims=True), (tile, LANES))
            for r0 in range(0, tile, SOFTMAX_ROWS):
                rows = slice(r0, r0 + SOFTMAX_ROWS)
                ls = jnp.zeros((SOFTMAX_ROWS, LANES), F32)
                for kb in range(nk * lane_blocks):
                    p = jnp.exp2(s_ref[c, rows, kb * LANES:(kb + 1) * LANES] - m[rows])
                    ls = ls + p
                    p_ref[c, rows, kb * LANES:(kb + 1) * LANES] = p.astype(BF16)
                l_ref[c, rows, :] = jnp.broadcast_to(jnp.sum(ls, axis=1, keepdims=True), (SOFTMAX_ROWS, LANES))

    def values(i, p_ref, l_ref):
        reps = ATT_V_DIM // LANES
        inv0 = 1.0 / l_ref[0]
        inv1 = lam_ref[0] / l_ref[1]
        o = (_dot(p_ref[0], v_ref[...]) * jnp.concatenate([inv0] * reps, axis=1)
             - _dot(p_ref[1], v_ref[...]) * jnp.concatenate([inv1] * reps, axis=1))
        ms = jnp.mean(o * o, axis=-1, keepdims=True)
        o_ref[tile_rows(i), :] = (o * lax.rsqrt(ms + EPS) * g_ref[...] * (1.0 - LAM_INIT)).astype(BF16)

    slot_a = (sa_ref, pa_ref, la_ref)
    slot_b = (sb_ref, pb_ref, lb_ref)
    scores(0, *slot_a)

    def pair(j, carry):
        scores(2 * j + 1, *slot_b)
        values(2 * j, *slot_a[1:])
        scores(2 * j + 2, *slot_a)
        values(2 * j + 1, *slot_b[1:])
        return carry

    lax.fori_loop(0, nq // 2 - 1, pair, 0)
    scores(nq - 1, *slot_b)
    values(nq - 2, *slot_a[1:])
    values(nq - 1, *slot_b[1:])


def _attention(proj3, bias_tiles, subln_g, lam, tile):
    bt, l, _ = proj3.shape
    nk = l // tile
    blk = ATT_V_DIM
    seq = lambda c0: pl.BlockSpec((None, l, blk), lambda b, h: (b, 0, c0 // blk + h))
    return pl.pallas_call(
        functools.partial(_attn_kernel, tile=tile, nk=nk),
        grid=(bt, ATT_HEADS),
        in_specs=[
            pl.BlockSpec(memory_space=pltpu.SMEM),
            seq(COL_Q), seq(COL_K), seq(COL_V),
            pl.BlockSpec((None, 5, tile, tile), lambda b, h: (h, 0, 0, 0), pipeline_mode=pl.Buffered(1)),
            pl.BlockSpec((1, blk), lambda b, h: (0, 0)),
        ],
        out_specs=seq(0),
        out_shape=jax.ShapeDtypeStruct((bt, l, ATT_WIDTH), BF16),
        scratch_shapes=[
            pltpu.VMEM((2, tile, l), F32),
            pltpu.VMEM((2, tile, l), F32),
            pltpu.VMEM((2, tile, l), BF16),
            pltpu.VMEM((2, tile, l), BF16),
            pltpu.VMEM((2, tile, LANES), F32),
            pltpu.VMEM((2, tile, LANES), F32),
        ],
        compiler_params=_cparams(("parallel", "parallel")),
        name="attention",
    )(lam, proj3, proj3, proj3, bias_tiles, subln_g)


def _branches_kernel(yf_ref, yb_ref, z_ref, og_ref, gates_ref, gw_ref, gb_ref, ws_ref, wa_ref, o_ref):
    y = jnp.concatenate([yf_ref[s] + yb_ref[s] for s in range(SLABS)], axis=-1)
    y = jax.nn.gelu(y)
    y = y * jax.nn.sigmoid(_dot(y.astype(BF16), gw_ref[...]) + gb_ref[...])
    sz = z_ref[:, :SSM_WIDTH].astype(F32)
    y = (y * (sz * jax.nn.sigmoid(sz))).astype(BF16)
    y_s = _dot(y, ws_ref[...])
    az = z_ref[:, SSM_WIDTH:].astype(F32)
    o = (og_ref[...].astype(F32) * (az * jax.nn.sigmoid(az))).astype(BF16)
    y_a = _dot(o, wa_ref[...])
    g_s = gates_ref[:, :D_MODEL].astype(F32)
    g_a = gates_ref[:, D_MODEL:].astype(F32)
    o_ref[...] = (jax.nn.sigmoid(g_s) * y_s + jax.nn.sigmoid(g_a) * y_a).astype(BF16)


def _branches(yf, yb, proj, og, glu_w, glu_b, w_s, w_a, tm):
    t = proj.shape[0]
    slab = pl.BlockSpec((SLABS, tm, LANES), lambda i: (0, i, 0))
    return pl.pallas_call(
        _branches_kernel,
        grid=(t // tm,),
        in_specs=[
            slab, slab,
            pl.BlockSpec((tm, SSM_WIDTH + ATT_WIDTH), lambda i: (i, COL_SZ // (SSM_WIDTH + ATT_WIDTH))),
            pl.BlockSpec((tm, ATT_WIDTH), lambda i: (i, 0)),
            pl.BlockSpec((tm, 2 * D_MODEL), lambda i: (i, COL_GATES // (2 * D_MODEL))),
            _const_spec((SSM_WIDTH, SSM_WIDTH)),
            _const_spec((1, SSM_WIDTH)),
            _const_spec((SSM_WIDTH, D_MODEL)),
            _const_spec((ATT_WIDTH, D_MODEL)),
        ],
        out_specs=pl.BlockSpec((tm, D_MODEL), lambda i: (i, 0)),
        out_shape=jax.ShapeDtypeStruct((t, D_MODEL), BF16),
        compiler_params=_cparams(("parallel",)),
        name="branches",
    )(yf, yb, proj, og, proj, glu_w, glu_b, w_s, w_a)


def _rms(x, g):
    ms = jnp.mean(x * x, axis=-1, keepdims=True)
    return x * lax.rsqrt(ms + EPS) * g


def _tail_kernel(m_ref, x_ref, p_ref, wo_ref, png_ref, pgw_ref, ppw_ref, fg_ref, o_ref):
    h = x_ref[...] + _dot(m_ref[...], wo_ref[...])
    gate = jax.nn.sigmoid(_dot(_rms(h, png_ref[...]).astype(BF16), pgw_ref[...]))
    h = h + gate * _dot(p_ref[...].astype(BF16), ppw_ref[...])
    o_ref[...] = _rms(h, fg_ref[...])


def _tail(merged, x2d, p2d, w_out, ple_norm_g, ple_gate_w, ple_proj_w, final_g, tm):
    t = x2d.shape[0]
    row = lambda w: pl.BlockSpec((tm, w), lambda i: (i, 0))
    return pl.pallas_call(
        _tail_kernel,
        grid=(t // tm,),
        in_specs=[
            row(D_MODEL), row(D_MODEL), row(PLE_DIM),
            _const_spec((D_MODEL, D_MODEL)),
            _const_spec((1, D_MODEL)),
            _const_spec((D_MODEL, D_MODEL)),
            _const_spec((PLE_DIM, D_MODEL)),
            _const_spec((1, D_MODEL)),
        ],
        out_specs=row(D_MODEL),
        out_shape=jax.ShapeDtypeStruct((t, D_MODEL), F32),
        compiler_params=_cparams(("parallel",)),
        name="tail",
    )(merged, x2d, p2d, w_out, ple_norm_g, ple_gate_w, ple_proj_w, final_g)


def _prep_w_in(w):
    c = np.cumsum([0, SSM_WIDTH, SSM_WIDTH, ATT_QK_WIDTH, ATT_QK_WIDTH, ATT_WIDTH, ATT_WIDTH, D_MODEL, D_MODEL])
    part = lambda k: w[:, c[k]:c[k + 1]]
    return jnp.concatenate([part(6), part(7), part(1), part(5), part(2), part(3), part(4), part(0)],
                           axis=1).astype(BF16)


def _tiles(bt, l):
    t = bt * l
    tm_in = 1024 if t % 1024 == 0 else l
    tm = 512 if t % 512 == 0 else l
    nb = 8 if bt % 8 == 0 else (4 if bt % 4 == 0 else (2 if bt % 2 == 0 else 1))
    att = 256 if l % 256 == 0 else MAX_DISTANCE
    return tm_in, tm, nb, att


def _trunk(x, p, w):
    bt, l, _ = x.shape
    t = bt * l
    tm_in, tm, nb, att_tile = _tiles(bt, l)
    x2d = x.reshape(t, D_MODEL)
    proj, sx = _inproj(x2d, w['norm_g'], w['w_in'], tm_in)
    yf, yb = _s5(sx, w['s5'], bt, l, nb)
    og = _attention(proj.reshape(bt, l, PROJ_W), _bias_tiles(w['rel_bias'], att_tile), w['subln_g'],
                    w['lam'], att_tile)
    merged = _branches(yf, yb, proj, og.reshape(t, ATT_WIDTH), w['glu_w'], w['glu_b'],
                       w['w_branch_s'], w['w_branch_a'], tm)
    out = _tail(merged, x2d, p.reshape(t, PLE_DIM), w['w_out'], w['ple_norm_g'], w['ple_gate_w'],
                w['ple_proj_w'], w['final_g'], tm)
    return out.reshape(bt, l, D_MODEL)


def _prep_weights(rel_bias, norm_g, w_in, ssm_lambda_re, ssm_lambda_im, ssm_log_dt, ssm_b_re, ssm_b_im,
                  ssm_c_re, ssm_c_im, ssm_d, glu_w, glu_b, lam_q1, lam_k1, lam_q2, lam_k2, subln_g,
                  w_branch_s, w_branch_a, w_out, ple_norm_g, ple_gate_w, ple_proj_w, final_g):
    lam = (jnp.exp(jnp.sum(lam_q1[0] * lam_k1[0])) - jnp.exp(jnp.sum(lam_q2[0] * lam_k2[0])) + LAM_INIT)
    return {
        'rel_bias': rel_bias,
        'norm_g': norm_g[0].reshape(1, D_MODEL),
        'w_in': _prep_w_in(w_in[0]),
        's5': _s5_weights(ssm_lambda_re[0], ssm_lambda_im[0], ssm_log_dt[0], ssm_b_re[0], ssm_b_im[0],
                          ssm_c_re[0], ssm_c_im[0], ssm_d[0]),
        'glu_w': glu_w[0].astype(BF16),
        'glu_b': glu_b[0].reshape(1, SSM_WIDTH),
        'lam': lam.reshape(1).astype(F32),
        'subln_g': subln_g[0].reshape(1, ATT_V_DIM),
        'w_branch_s': w_branch_s[0].astype(BF16),
        'w_branch_a': w_branch_a[0].astype(BF16),
        'w_out': w_out[0].astype(BF16),
        'ple_norm_g': ple_norm_g[0].reshape(1, D_MODEL),
        'ple_gate_w': ple_gate_w[0].astype(BF16),
        'ple_proj_w': ple_proj_w[0].astype(BF16),
        'final_g': final_g.reshape(1, D_MODEL),
    }


def kernel(x_prompt, x_sample, p_prompt, p_sample, rel_bias, norm_g, w_in, ssm_lambda_re, ssm_lambda_im,
           ssm_log_dt, ssm_b_re, ssm_b_im, ssm_c_re, ssm_c_im, ssm_d, glu_w, glu_b, lam_q1, lam_k1, lam_q2,
           lam_k2, subln_g, w_branch_s, w_branch_a, w_out, ple_norm_g, ple_gate_w, ple_proj_w, final_g):
    w = _prep_weights(rel_bias, norm_g, w_in, ssm_lambda_re, ssm_lambda_im, ssm_log_dt, ssm_b_re, ssm_b_im,
                      ssm_c_re, ssm_c_im, ssm_d, glu_w, glu_b, lam_q1, lam_k1, lam_q2, lam_k2, subln_g,
                      w_branch_s, w_branch_a, w_out, ple_norm_g, ple_gate_w, ple_proj_w, final_g)
    return (_trunk(x_prompt, p_prompt[0], w), _trunk(x_sample, p_sample[0], w))
```

```python
import functools
import math

import jax
import jax.numpy as jnp
import numpy as np
from jax import lax
from jax.experimental import pallas as pl
from jax.experimental.pallas import tpu as pltpu

F32 = jnp.float32
BF16 = jnp.bfloat16

D_MODEL = 2048
SSM_WIDTH = 1024
SSM_GROUP = 16
SSM_GROUPS = SSM_WIDTH // SSM_GROUP
SSM_STATE = 64
ATT_HEADS = 4
ATT_HEAD_DIM = 128
ATT_V_DIM = 2 * ATT_HEAD_DIM
ATT_QK_WIDTH = ATT_HEADS * 2 * ATT_HEAD_DIM
ATT_WIDTH = ATT_HEADS * ATT_V_DIM
NUM_BUCKETS = 32
MAX_DISTANCE = 128
PLE_DIM = 256
EPS = 1e-6
LAM_INIT = 0.8 - 0.6 * math.exp(-0.3 * 0)
LOG2E = math.log2(math.e)
ATT_SCALE = ATT_HEAD_DIM ** -0.5 * LOG2E

LANES = 128
SUBLANES = 8
SLABS = SSM_WIDTH // LANES
SLAB_GROUPS = LANES // SSM_GROUP
S5_CHUNK = 8
SLAB_STATE = SLAB_GROUPS * SSM_STATE
S5_K = S5_CHUNK * LANES
S5_ROWS = 512

IN_WIDTH = 2 * SSM_WIDTH + 2 * ATT_QK_WIDTH + 2 * ATT_WIDTH + 2 * D_MODEL
PROJ_W = IN_WIDTH - SSM_WIDTH
COL_GATES = 0
COL_SZ = 4096
COL_AZ = 5120
COL_Q = 6144
COL_K = 7168
COL_V = 8192
IN_TN = 1024
Q_TILE_INDEX = COL_Q // IN_TN
VMEM_LIMIT = 56 * 1024 * 1024


def _cparams(sem):
    return pltpu.CompilerParams(dimension_semantics=sem, vmem_limit_bytes=VMEM_LIMIT)


def _dot(a, b):
    return jnp.dot(a, b, preferred_element_type=F32)


def _const_spec(shape):
    zeros = (0,) * len(shape)
    return pl.BlockSpec(shape, lambda *_: zeros, pipeline_mode=pl.Buffered(1))


def _inproj_kernel(x_ref, g_ref, w_ref, proj_ref, sx_ref, hn_ref):
    j = pl.program_id(1)
    last = pl.num_programs(1) - 1

    @pl.when(j == 0)
    def _():
        x = x_ref[...]
        ms = jnp.mean(x * x, axis=-1, keepdims=True)
        hn_ref[...] = (x * lax.rsqrt(ms + EPS) * g_ref[...]).astype(BF16)

    acc = _dot(hn_ref[...], w_ref[...])

    @pl.when(j < last)
    def _():
        scale = jnp.where(j == Q_TILE_INDEX, ATT_SCALE, 1.0).astype(F32)
        proj_ref[...] = (acc * scale).astype(BF16)

    @pl.when(j == last)
    def _():
        for s in range(SLABS):
            sx_ref[s] = acc[:, s * LANES:(s + 1) * LANES]


def _inproj(x2d, norm_g, w_in, tm):
    t = x2d.shape[0]
    nj = w_in.shape[1] // IN_TN
    return pl.pallas_call(
        _inproj_kernel,
        grid=(t // tm, nj),
        in_specs=[
            pl.BlockSpec((tm, D_MODEL), lambda i, j: (i, 0)),
            pl.BlockSpec((1, D_MODEL), lambda i, j: (0, 0)),
            pl.BlockSpec((D_MODEL, IN_TN), lambda i, j: (0, j)),
        ],
        out_specs=[
            pl.BlockSpec((tm, IN_TN), lambda i, j: (i, jnp.minimum(j, nj - 2))),
            pl.BlockSpec((SLABS, tm, LANES), lambda i, j: (0, i, 0)),
        ],
        out_shape=[
            jax.ShapeDtypeStruct((t, PROJ_W), BF16),
            jax.ShapeDtypeStruct((SLABS, t, LANES), F32),
        ],
        scratch_shapes=[pltpu.VMEM((tm, D_MODEL), BF16)],
        compiler_params=_cparams(("parallel", "arbitrary")),
        name="inproj",
    )(x2d, norm_g, w_in)


def _s5_kernel(xf_ref, xb_ref, wsf_ref, wsb_ref, wi_ref, wof_ref, wob_ref, af_ref, ab_ref,
               yf_ref, yb_ref, xs_ref, sf_ref, sb_ref, ys_ref, carry_ref, *, nb, rb):
    i = pl.program_id(2)
    half = SLAB_STATE

    @pl.when(i == 0)
    def _():
        carry_ref[...] = jnp.zeros_like(carry_ref)

    def gather_rows(x_ref):
        for j in range(S5_CHUNK):
            for b in range(nb):
                xs_ref[j, pl.ds(b, rb, stride=nb), :] = x_ref[b, pl.ds(j, rb, stride=S5_CHUNK), :]
        return jnp.concatenate([xs_ref[j] for j in range(S5_CHUNK)], axis=-1).astype(BF16)

    def scatter_rows(y, y_ref):
        for j in range(S5_CHUNK):
            ys_ref[j] = y[:, j * LANES:(j + 1) * LANES]
        for j in range(S5_CHUNK):
            for b in range(nb):
                y_ref[b, pl.ds(j, rb, stride=S5_CHUNK), :] = ys_ref[j, pl.ds(b, rb, stride=nb), :]

    def scan(s_ref, a_ref, slot, reverse):
        a_re = jnp.broadcast_to(a_ref[:, :half], (nb, half))
        a_im = jnp.broadcast_to(a_ref[:, half:], (nb, half))

        per = max(1, SUBLANES // nb)
        rows = per * nb
        steps = rb // per

        def body(step, carry):
            e_re, e_im = carry
            k = (steps - 1 - step) if reverse else step
            row = pl.multiple_of(k * rows, rows)
            s = s_ref[pl.ds(row, rows), :]
            entering = [None] * per
            for q in (range(per - 1, -1, -1) if reverse else range(per)):
                sq = s[q * nb:(q + 1) * nb]
                entering[q] = jnp.concatenate([e_re, e_im], axis=-1)
                e_re, e_im = (a_re * e_re - a_im * e_im + sq[:, :half],
                              a_re * e_im + a_im * e_re + sq[:, half:])
            s_ref[pl.ds(row, rows), :] = jnp.concatenate(entering, axis=0)
            return e_re, e_im

        c0 = carry_ref[slot]
        e_re, e_im = lax.fori_loop(0, steps, body, (c0[:, :half], c0[:, half:]), unroll=4)
        carry_ref[slot] = jnp.concatenate([e_re, e_im], axis=-1)

    x_f = gather_rows(xf_ref)
    sf_ref[...] = _dot(x_f, wsf_ref[...])
    x_b = gather_rows(xb_ref)
    sb_ref[...] = _dot(x_b, wsb_ref[...])
    scan(sf_ref, af_ref, 0, False)
    scan(sb_ref, ab_ref, 1, True)
    y_f = _dot(x_f, wi_ref[...]) + _dot(sf_ref[...].astype(BF16), wof_ref[...])
    scatter_rows(y_f, yf_ref)
    y_b = _dot(sb_ref[...].astype(BF16), wob_ref[...])
    scatter_rows(y_b, yb_ref)


def _s5(sx, s5w, bt, l, nb):
    lb = min(S5_ROWS * S5_CHUNK // nb, l)
    rb = lb // S5_CHUNK
    nl = l // lb
    sx4 = sx.reshape(SLABS, bt, l, LANES)
    w_in, w_intra, w_out, a = s5w
    wspec = lambda k: pl.BlockSpec((None, None, S5_K, S5_K), lambda s, g, i: (k, s, 0, 0))
    aspec = lambda k: pl.BlockSpec((None, None, 1, 2 * SLAB_STATE), lambda s, g, i: (k, s, 0, 0))
    fwd = pl.BlockSpec((None, nb, lb, LANES), lambda s, g, i: (s, g, i, 0))
    bwd = pl.BlockSpec((None, nb, lb, LANES), lambda s, g, i: (s, g, nl - 1 - i, 0))
    m = rb * nb
    yf, yb = pl.pallas_call(
        functools.partial(_s5_kernel, nb=nb, rb=rb),
        grid=(SLABS, bt // nb, nl),
        in_specs=[fwd, bwd, wspec(0), wspec(1), wspec(0), wspec(0), wspec(1), aspec(0), aspec(1)],
        out_specs=[fwd, bwd],
        out_shape=[jax.ShapeDtypeStruct(sx4.shape, F32)] * 2,
        scratch_shapes=[
            pltpu.VMEM((S5_CHUNK, m, LANES), F32),
            pltpu.VMEM((m, 2 * SLAB_STATE), F32),
            pltpu.VMEM((m, 2 * SLAB_STATE), F32),
            pltpu.VMEM((S5_CHUNK, m, LANES), F32),
            pltpu.VMEM((2, nb, 2 * SLAB_STATE), F32),
        ],
        compiler_params=_cparams(("parallel", "parallel", "arbitrary")),
        name="s5",
    )(sx4, sx4, w_in, w_in, w_intra, w_out, w_out, a, a)
    return yf.reshape(sx.shape), yb.reshape(sx.shape)


def _s5_direction_terms(lam_re, lam_im, log_dt, b_re, b_im, c_re, c_im):
    hi = lax.Precision.HIGHEST
    dt = jnp.exp(log_dt)[:, None]
    mag = jnp.exp(lam_re * dt)
    a_re = mag * jnp.cos(lam_im * dt)
    a_im = mag * jnp.sin(lam_im * dt)
    den = lam_re * lam_re + lam_im * lam_im
    nr = a_re - 1.0
    k_re = (nr * lam_re + a_im * lam_im) / den
    k_im = (a_im * lam_re - nr * lam_im) / den
    bb_re = k_re[..., None] * b_re - k_im[..., None] * b_im
    bb_im = k_re[..., None] * b_im + k_im[..., None] * b_re
    tau = jnp.arange(S5_CHUNK + 1, dtype=F32)[:, None, None]
    pmag = jnp.exp(lam_re * dt * tau)
    p_re = pmag * jnp.cos(lam_im * dt * tau)
    p_im = pmag * jnp.sin(lam_im * dt * tau)
    cp_re = c_re[None] * p_re[:, :, None, :] - c_im[None] * p_im[:, :, None, :]
    cp_im = c_re[None] * p_im[:, :, None, :] + c_im[None] * p_re[:, :, None, :]
    pb_re = p_re[..., None] * bb_re[None] - p_im[..., None] * bb_im[None]
    pb_im = p_re[..., None] * bb_im[None] + p_im[..., None] * bb_re[None]
    kern = (jnp.einsum('tgdn,gnc->tgdc', cp_re[:-1], bb_re, precision=hi)
            - jnp.einsum('tgdn,gnc->tgdc', cp_im[:-1], bb_im, precision=hi))
    return p_re, p_im, cp_re, cp_im, pb_re, pb_im, kern


def _slab_blockdiag(w):
    nb, _, r1, r2, c1, c2 = w.shape
    nrow, ncol = r1 * SLAB_GROUPS * r2, c1 * SLAB_GROUPS * c2
    w = w.astype(BF16).reshape(nb, SLABS, SLAB_GROUPS, r1, r2, c1 * c2)
    rows = jnp.transpose(w, (0, 1, 3, 2, 4, 5)).reshape(nb * SLABS * nrow, c1 * c2)
    col = jnp.arange(ncol, dtype=jnp.int32)
    src = (col // (SLAB_GROUPS * c2)) * c2 + col % c2
    spread = (jnp.arange(c1 * c2, dtype=jnp.int32)[:, None] == src[None, :]).astype(BF16)
    dense = jnp.dot(rows, spread, preferred_element_type=BF16).reshape(nb, SLABS, nrow, ncol)
    row_g = (jnp.arange(nrow, dtype=jnp.int32) // r2) % SLAB_GROUPS
    col_h = (col // c2) % SLAB_GROUPS
    return jnp.where((row_g[:, None] == col_h[None, :])[None, None], dense, jnp.zeros((), BF16))


def _s5_weights(lam_re, lam_im, log_dt, b_re, b_im, c_re, c_im, d):
    tc = S5_CHUNK
    p_re, p_im, cp_re, cp_im, pb_re, pb_im, kern = jax.vmap(_s5_direction_terms)(
        lam_re, lam_im, log_dt, b_re, b_im, c_re, c_im)
    direction = np.arange(2)[:, None]
    steps = np.arange(tc)
    sel = np.stack([tc - 1 - steps, steps])
    w = jnp.stack([pb_re[direction, sel], pb_im[direction, sel]], axis=1)
    w_in = _slab_blockdiag(jnp.transpose(w, (0, 3, 2, 5, 1, 4)))
    lag = steps[None, :] - steps[:, None]
    kf = kern[0][np.clip(lag, 0, tc - 1)] * jnp.asarray(lag >= 0, F32)[:, :, None, None, None]
    kb = kern[1][np.clip(-lag, 0, tc - 1)] * jnp.asarray(lag <= 0, F32)[:, :, None, None, None]
    skip = (jnp.eye(tc, dtype=F32)[:, :, None, None, None]
            * (jnp.eye(SSM_GROUP, dtype=F32)[None, None, None] * d.reshape(SSM_GROUPS, 1, SSM_GROUP)[None, None]))
    wi = kf + kb + skip
    w_intra = _slab_blockdiag(jnp.transpose(wi, (2, 0, 4, 1, 3))[None])
    sel = np.stack([steps + 1, tc - steps])
    w = jnp.stack([cp_re[direction, sel], -cp_im[direction, sel]], axis=1)
    w_out = _slab_blockdiag(jnp.transpose(w, (0, 3, 1, 5, 2, 4)))
    a = jnp.stack([p_re[:, tc], p_im[:, tc]], axis=1).reshape(2, 2, SLABS, SLAB_STATE)
    a = jnp.transpose(a, (0, 2, 1, 3)).reshape(2, SLABS, 1, 2 * SLAB_STATE)
    return w_in, w_intra, w_out, a


def _rel_bucket(rel):
    half = NUM_BUCKETS // 2
    max_exact = half // 2
    ret = (rel > 0).astype(jnp.int32) * half
    n = jnp.abs(rel)
    nf = jnp.maximum(n, 1).astype(jnp.float32)
    large = max_exact + (jnp.log(nf / max_exact) / math.log(MAX_DISTANCE / max_exact)
                         * (half - max_exact)).astype(jnp.int32)
    large = jnp.minimum(large, half - 1)
    return ret + jnp.where(n < max_exact, n, large)


def _bias_tiles(rel_bias, tile):
    assert tile >= MAX_DISTANCE
    qi = jnp.arange(tile, dtype=jnp.int32)[:, None]
    kj = jnp.arange(tile, dtype=jnp.int32)[None, :]
    offs = (jnp.arange(5, dtype=jnp.int32) - 2) * tile
    rel = offs[:, None, None] + kj[None] - qi[None]
    onehot = (_rel_bucket(rel)[None] == jnp.arange(NUM_BUCKETS, dtype=jnp.int32)[:, None, None, None])
    tab = jnp.einsum('nh,nabc->habc', rel_bias.astype(F32) * LOG2E, onehot.astype(F32),
                     precision=lax.Precision.HIGHEST)
    return tab


def _attn_kernel(lam_ref, q_ref, k_ref, v_ref, bias_ref, g_ref, o_ref,
                 sa_ref, sb_ref, ma_ref, mb_ref, *, tile, nk):
    hd = ATT_HEAD_DIM
    nt = (((1,), (1,)), ((), ()))
    lane_blocks = tile // LANES
    reps = ATT_V_DIM // LANES
    nq = nk
    assert nq % 2 == 0

    def tile_rows(i):
        return pl.ds(pl.multiple_of(i * tile, tile), tile)

    def step(i_qk, i_pv, w_bufs, r_bufs):
        if i_qk is not None:
            s_w, m_w = w_bufs
            q_next = [q_ref[tile_rows(i_qk), c * hd:(c + 1) * hd] for c in range(2)]
            mx = [jnp.full((tile, LANES), -jnp.inf, F32) for _ in range(2)]
        if i_pv is not None:
            s_r, m_r = r_bufs
            m_cur = [m_r[c] for c in range(2)]
            ls = [jnp.zeros((tile, LANES), F32) for _ in range(2)]
            acc = [None, None]
        for kj in range(nk):
            cols = slice(kj * tile, (kj + 1) * tile)
            if i_qk is not None:
                bias = bias_ref[jnp.clip(kj - i_qk, -2, 2) + 2]
                for c in range(2):
                    s = lax.dot_general(q_next[c], k_ref[cols, c * hd:(c + 1) * hd], nt,
                                        preferred_element_type=F32) + bias
                    s_w[c, :, cols] = s
                    for b in range(lane_blocks):
                        mx[c] = jnp.maximum(mx[c], s[:, b * LANES:(b + 1) * LANES])
            if i_pv is not None:
                vt = v_ref[cols, :]
                for c in range(2):
                    ps = []
                    for b in range(lane_blocks):
                        lo = kj * tile + b * LANES
                        p = jnp.exp2(s_r[c, :, lo:lo + LANES] - m_cur[c])
                        ls[c] = ls[c] + p
                        ps.append(p.astype(BF16))
                    d = _dot(jnp.concatenate(ps, axis=1), vt)
                    acc[c] = d if acc[c] is None else acc[c] + d
        if i_qk is not None:
            for c in range(2):
                m_w[c] = jnp.broadcast_to(jnp.max(mx[c], axis=1, keepdims=True), (tile, LANES))
        if i_pv is not None:
            inv0 = jnp.broadcast_to(1.0 / jnp.sum(ls[0], axis=1, keepdims=True), (tile, LANES))
            inv1 = jnp.broadcast_to(lam_ref[0] / jnp.sum(ls[1], axis=1, keepdims=True), (tile, LANES))
            o = (acc[0] * jnp.concatenate([inv0] * reps, axis=1)
                 - acc[1] * jnp.concatenate([inv1] * reps, axis=1))
            ms = jnp.mean(o * o, axis=-1, keepdims=True)
            o_ref[tile_rows(i_pv), :] = (o * lax.rsqrt(ms + EPS) * g_ref[...] * (1.0 - LAM_INIT)).astype(BF16)

    buf_a = (sa_ref, ma_ref)
    buf_b = (sb_ref, mb_ref)
    step(0, None, buf_a, None)

    def pair(j, carry):
        step(2 * j + 1, 2 * j, buf_b, buf_a)
        step(2 * j + 2, 2 * j + 1, buf_a, buf_b)
        return carry

    lax.fori_loop(0, nq // 2 - 1, pair, 0)
    step(nq - 1, nq - 2, buf_b, buf_a)
    step(None, nq - 1, None, buf_b)


def _attention(proj3, bias_tiles, subln_g, lam, tile):
    bt, l, _ = proj3.shape
    nk = l // tile
    blk = ATT_V_DIM
    seq = lambda c0: pl.BlockSpec((None, l, blk), lambda b, h: (b, 0, c0 // blk + h))
    return pl.pallas_call(
        functools.partial(_attn_kernel, tile=tile, nk=nk),
        grid=(bt, ATT_HEADS),
        in_specs=[
            pl.BlockSpec(memory_space=pltpu.SMEM),
            seq(COL_Q), seq(COL_K), seq(COL_V),
            pl.BlockSpec((None, 5, tile, tile), lambda b, h: (h, 0, 0, 0), pipeline_mode=pl.Buffered(1)),
            pl.BlockSpec((1, blk), lambda b, h: (0, 0)),
        ],
        out_specs=seq(0),
        out_shape=jax.ShapeDtypeStruct((bt, l, ATT_WIDTH), BF16),
        scratch_shapes=[
            pltpu.VMEM((2, tile, l), F32),
            pltpu.VMEM((2, tile, l), F32),
            pltpu.VMEM((2, tile, LANES), F32),
            pltpu.VMEM((2, tile, LANES), F32),
        ],
        compiler_params=_cparams(("parallel", "parallel")),
        name="attention",
    )(lam, proj3, proj3, proj3, bias_tiles, subln_g)


def _branches_kernel(yf_ref, yb_ref, z_ref, og_ref, gates_ref, gw_ref, gb_ref, ws_ref, wa_ref, o_ref):
    y = jnp.concatenate([yf_ref[s] + yb_ref[s] for s in range(SLABS)], axis=-1)
    y = jax.nn.gelu(y)
    y = y * jax.nn.sigmoid(_dot(y.astype(BF16), gw_ref[...]) + gb_ref[...])
    sz = z_ref[:, :SSM_WIDTH].astype(F32)
    y = (y * (sz * jax.nn.sigmoid(sz))).astype(BF16)
    y_s = _dot(y, ws_ref[...])
    az = z_ref[:, SSM_WIDTH:].astype(F32)
    o = (og_ref[...].astype(F32) * (az * jax.nn.sigmoid(az))).astype(BF16)
    y_a = _dot(o, wa_ref[...])
    g_s = gates_ref[:, :D_MODEL].astype(F32)
    g_a = gates_ref[:, D_MODEL:].astype(F32)
    o_ref[...] = (jax.nn.sigmoid(g_s) * y_s + jax.nn.sigmoid(g_a) * y_a).astype(BF16)


def _branches(yf, yb, proj, og, glu_w, glu_b, w_s, w_a, tm):
    t = proj.shape[0]
    slab = pl.BlockSpec((SLABS, tm, LANES), lambda i: (0, i, 0))
    return pl.pallas_call(
        _branches_kernel,
        grid=(t // tm,),
        in_specs=[
            slab, slab,
            pl.BlockSpec((tm, SSM_WIDTH + ATT_WIDTH), lambda i: (i, COL_SZ // (SSM_WIDTH + ATT_WIDTH))),
            pl.BlockSpec((tm, ATT_WIDTH), lambda i: (i, 0)),
            pl.BlockSpec((tm, 2 * D_MODEL), lambda i: (i, COL_GATES // (2 * D_MODEL))),
            _const_spec((SSM_WIDTH, SSM_WIDTH)),
            _const_spec((1, SSM_WIDTH)),
            _const_spec((SSM_WIDTH, D_MODEL)),
            _const_spec((ATT_WIDTH, D_MODEL)),
        ],
        out_specs=pl.BlockSpec((tm, D_MODEL), lambda i: (i, 0)),
        out_shape=jax.ShapeDtypeStruct((t, D_MODEL), BF16),
        compiler_params=_cparams(("parallel",)),
        name="branches",
    )(yf, yb, proj, og, proj, glu_w, glu_b, w_s, w_a)


def _rms(x, g):
    ms = jnp.mean(x * x, axis=-1, keepdims=True)
    return x * lax.rsqrt(ms + EPS) * g


def _tail_kernel(m_ref, x_ref, p_ref, wo_ref, png_ref, pgw_ref, ppw_ref, fg_ref, o_ref):
    h = x_ref[...] + _dot(m_ref[...], wo_ref[...])
    gate = jax.nn.sigmoid(_dot(_rms(h, png_ref[...]).astype(BF16), pgw_ref[...]))
    h = h + gate * _dot(p_ref[...].astype(BF16), ppw_ref[...])
    o_ref[...] = _rms(h, fg_ref[...])


def _tail(merged, x2d, p2d, w_out, ple_norm_g, ple_gate_w, ple_proj_w, final_g, tm):
    t = x2d.shape[0]
    row = lambda w: pl.BlockSpec((tm, w), lambda i: (i, 0))
    return pl.pallas_call(
        _tail_kernel,
        grid=(t // tm,),
        in_specs=[
            row(D_MODEL), row(D_MODEL), row(PLE_DIM),
            _const_spec((D_MODEL, D_MODEL)),
            _const_spec((1, D_MODEL)),
            _const_spec((D_MODEL, D_MODEL)),
            _const_spec((PLE_DIM, D_MODEL)),
            _const_spec((1, D_MODEL)),
        ],
        out_specs=row(D_MODEL),
        out_shape=jax.ShapeDtypeStruct((t, D_MODEL), F32),
        compiler_params=_cparams(("parallel",)),
        name="tail",
    )(merged, x2d, p2d, w_out, ple_norm_g, ple_gate_w, ple_proj_w, final_g)


def _prep_w_in(w):
    c = np.cumsum([0, SSM_WIDTH, SSM_WIDTH, ATT_QK_WIDTH, ATT_QK_WIDTH, ATT_WIDTH, ATT_WIDTH, D_MODEL, D_MODEL])
    part = lambda k: w[:, c[k]:c[k + 1]]
    return jnp.concatenate([part(6), part(7), part(1), part(5), part(2), part(3), part(4), part(0)],
                           axis=1).astype(BF16)


def _tiles(bt, l):
    t = bt * l
    tm_in = 1024 if t % 1024 == 0 else l
    tm = 512 if t % 512 == 0 else l
    nb = 8 if bt % 8 == 0 else (4 if bt % 4 == 0 else (2 if bt % 2 == 0 else 1))
    att = 256 if l % 256 == 0 else MAX_DISTANCE
    return tm_in, tm, nb, att


def _trunk(x, p, w):
    bt, l, _ = x.shape
    t = bt * l
    tm_in, tm, nb, att_tile = _tiles(bt, l)
    x2d = x.reshape(t, D_MODEL)
    proj, sx = _inproj(x2d, w['norm_g'], w['w_in'], tm_in)
    yf, yb = _s5(sx, w['s5'], bt, l, nb)
    og = _attention(proj.reshape(bt, l, PROJ_W), _bias_tiles(w['rel_bias'], att_tile), w['subln_g'],
                    w['lam'], att_tile)
    merged = _branches(yf, yb, proj, og.reshape(t, ATT_WIDTH), w['glu_w'], w['glu_b'],
                       w['w_branch_s'], w['w_branch_a'], tm)
    out = _tail(merged, x2d, p.reshape(t, PLE_DIM), w['w_out'], w['ple_norm_g'], w['ple_gate_w'],
                w['ple_proj_w'], w['final_g'], tm)
    return out.reshape(bt, l, D_MODEL)


def _prep_weights(rel_bias, norm_g, w_in, ssm_lambda_re, ssm_lambda_im, ssm_log_dt, ssm_b_re, ssm_b_im,
                  ssm_c_re, ssm_c_im, ssm_d, glu_w, glu_b, lam_q1, lam_k1, lam_q2, lam_k2, subln_g,
                  w_branch_s, w_branch_a, w_out, ple_norm_g, ple_gate_w, ple_proj_w, final_g):
    lam = (jnp.exp(jnp.sum(lam_q1[0] * lam_k1[0])) - jnp.exp(jnp.sum(lam_q2[0] * lam_k2[0])) + LAM_INIT)
    return {
        'rel_bias': rel_bias,
        'norm_g': norm_g[0].reshape(1, D_MODEL),
        'w_in': _prep_w_in(w_in[0]),
        's5': _s5_weights(ssm_lambda_re[0], ssm_lambda_im[0], ssm_log_dt[0], ssm_b_re[0], ssm_b_im[0],
                          ssm_c_re[0], ssm_c_im[0], ssm_d[0]),
        'glu_w': glu_w[0].astype(BF16),
        'glu_b': glu_b[0].reshape(1, SSM_WIDTH),
        'lam': lam.reshape(1).astype(F32),
        'subln_g': subln_g[0].reshape(1, ATT_V_DIM),
        'w_branch_s': w_branch_s[0].astype(BF16),
        'w_branch_a': w_branch_a[0].astype(BF16),
        'w_out': w_out[0].astype(BF16),
        'ple_norm_g': ple_norm_g[0].reshape(1, D_MODEL),
        'ple_gate_w': ple_gate_w[0].astype(BF16),
        'ple_proj_w': ple_proj_w[0].astype(BF16),
        'final_g': final_g.reshape(1, D_MODEL),
    }


def kernel(x_prompt, x_sample, p_prompt, p_sample, rel_bias, norm_g, w_in, ssm_lambda_re, ssm_lambda_im,
           ssm_log_dt, ssm_b_re, ssm_b_im, ssm_c_re, ssm_c_im, ssm_d, glu_w, glu_b, lam_q1, lam_k1, lam_q2,
           lam_k2, subln_g, w_branch_s, w_branch_a, w_out, ple_norm_g, ple_gate_w, ple_proj_w, final_g):
    w = _prep_weights(rel_bias, norm_g, w_in, ssm_lambda_re, ssm_lambda_im, ssm_log_dt, ssm_b_re, ssm_b_im,
                      ssm_c_re, ssm_c_im, ssm_d, glu_w, glu_b, lam_q1, lam_k1, lam_q2, lam_k2, subln_g,
                      w_branch_s, w_branch_a, w_out, ple_norm_g, ple_gate_w, ple_proj_w, final_g)
    return (_trunk(x_prompt, p_prompt[0], w), _trunk(x_sample, p_sample[0], w))
```

```python
import functools
import math

import jax
import jax.numpy as jnp
import numpy as np
from jax import lax
from jax.experimental import pallas as pl
from jax.experimental.pallas import tpu as pltpu

F32 = jnp.float32
BF16 = jnp.bfloat16

D_MODEL = 2048
SSM_WIDTH = 1024
SSM_GROUP = 16
SSM_GROUPS = SSM_WIDTH // SSM_GROUP
SSM_STATE = 64
ATT_HEADS = 4
ATT_HEAD_DIM = 128
ATT_V_DIM = 2 * ATT_HEAD_DIM
ATT_QK_WIDTH = ATT_HEADS * 2 * ATT_HEAD_DIM
ATT_WIDTH = ATT_HEADS * ATT_V_DIM
NUM_BUCKETS = 32
MAX_DISTANCE = 128
PLE_DIM = 256
EPS = 1e-6
LAM_INIT = 0.8 - 0.6 * math.exp(-0.3 * 0)
LOG2E = math.log2(math.e)
ATT_SCALE = ATT_HEAD_DIM ** -0.5 * LOG2E

LANES = 128
SUBLANES = 8
SLABS = SSM_WIDTH // LANES
SLAB_GROUPS = LANES // SSM_GROUP
S5_CHUNK = 8
SLAB_STATE = SLAB_GROUPS * SSM_STATE
S5_K = S5_CHUNK * LANES
S5_ROWS = 512

IN_WIDTH = 2 * SSM_WIDTH + 2 * ATT_QK_WIDTH + 2 * ATT_WIDTH + 2 * D_MODEL
PROJ_W = IN_WIDTH - SSM_WIDTH
COL_GATES = 0
COL_SZ = 4096
COL_AZ = 5120
COL_Q = 6144
COL_K = 7168
COL_V = 8192
IN_TN = 1024
Q_TILE_INDEX = COL_Q // IN_TN
VMEM_LIMIT = 56 * 1024 * 1024


def _cparams(sem):
    return pltpu.CompilerParams(dimension_semantics=sem, vmem_limit_bytes=VMEM_LIMIT)


def _dot(a, b):
    return jnp.dot(a, b, preferred_element_type=F32)


def _sigmoid(x):
    return 0.5 * jnp.tanh(0.5 * x) + 0.5


def _const_spec(shape):
    zeros = (0,) * len(shape)
    return pl.BlockSpec(shape, lambda *_: zeros, pipeline_mode=pl.Buffered(1))


def _inproj_kernel(x_ref, g_ref, w_ref, proj_ref, sx_ref, hn_ref):
    j = pl.program_id(1)
    last = pl.num_programs(1) - 1

    @pl.when(j == 0)
    def _():
        x = x_ref[...]
        ms = jnp.mean(x * x, axis=-1, keepdims=True)
        hn_ref[...] = (x * lax.rsqrt(ms + EPS) * g_ref[...]).astype(BF16)

    acc = _dot(hn_ref[...], w_ref[...])

    @pl.when(j < last)
    def _():
        scale = jnp.where(j == Q_TILE_INDEX, ATT_SCALE, 1.0).astype(F32)
        proj_ref[...] = (acc * scale).astype(BF16)

    @pl.when(j == last)
    def _():
        for s in range(SLABS):
            sx_ref[s] = acc[:, s * LANES:(s + 1) * LANES]


def _inproj(x2d, norm_g, w_in, tm):
    t = x2d.shape[0]
    nj = w_in.shape[1] // IN_TN
    return pl.pallas_call(
        _inproj_kernel,
        grid=(t // tm, nj),
        in_specs=[
            pl.BlockSpec((tm, D_MODEL), lambda i, j: (i, 0)),
            pl.BlockSpec((1, D_MODEL), lambda i, j: (0, 0)),
            pl.BlockSpec((D_MODEL, IN_TN), lambda i, j: (0, j)),
        ],
        out_specs=[
            pl.BlockSpec((tm, IN_TN), lambda i, j: (i, jnp.minimum(j, nj - 2))),
            pl.BlockSpec((SLABS, tm, LANES), lambda i, j: (0, i, 0)),
        ],
        out_shape=[
            jax.ShapeDtypeStruct((t, PROJ_W), BF16),
            jax.ShapeDtypeStruct((SLABS, t, LANES), F32),
        ],
        scratch_shapes=[pltpu.VMEM((tm, D_MODEL), BF16)],
        compiler_params=_cparams(("parallel", "arbitrary")),
        name="inproj",
    )(x2d, norm_g, w_in)


def _s5_kernel(xf_ref, xb_ref, wsf_ref, wsb_ref, wi_ref, wof_ref, wob_ref, af_ref, ab_ref,
               yf_ref, yb_ref, xs_ref, sf_ref, sb_ref, ys_ref, carry_ref, *, nb, rb):
    i = pl.program_id(2)
    half = SLAB_STATE

    @pl.when(i == 0)
    def _():
        carry_ref[...] = jnp.zeros_like(carry_ref)

    def gather_rows(x_ref):
        for j in range(S5_CHUNK):
            for b in range(nb):
                xs_ref[j, pl.ds(b, rb, stride=nb), :] = x_ref[b, pl.ds(j, rb, stride=S5_CHUNK), :]
        return jnp.concatenate([xs_ref[j] for j in range(S5_CHUNK)], axis=-1).astype(BF16)

    def scatter_rows(y, y_ref):
        for j in range(S5_CHUNK):
            ys_ref[j] = y[:, j * LANES:(j + 1) * LANES]
        for j in range(S5_CHUNK):
            for b in range(nb):
                y_ref[b, pl.ds(j, rb, stride=S5_CHUNK), :] = ys_ref[j, pl.ds(b, rb, stride=nb), :]

    def scan():
        per = max(1, SUBLANES // nb)
        rows = per * nb
        steps = rb // per
        dirs = ((sf_ref, af_ref, False), (sb_ref, ab_ref, True))
        poles = [(jnp.broadcast_to(a_ref[:, :half], (nb, half)), jnp.broadcast_to(a_ref[:, half:], (nb, half)))
                 for _, a_ref, _ in dirs]

        def body(step, carry):
            out = []
            for (s_ref, _, reverse), (a_re, a_im), (e_re, e_im) in zip(dirs, poles, carry):
                k = (steps - 1 - step) if reverse else step
                row = pl.multiple_of(k * rows, rows)
                s = s_ref[pl.ds(row, rows), :]
                entering = [None] * per
                for q in (range(per - 1, -1, -1) if reverse else range(per)):
                    sq = s[q * nb:(q + 1) * nb]
                    entering[q] = jnp.concatenate([e_re, e_im], axis=-1)
                    e_re, e_im = (a_re * e_re - a_im * e_im + sq[:, :half],
                                  a_re * e_im + a_im * e_re + sq[:, half:])
                s_ref[pl.ds(row, rows), :] = jnp.concatenate(entering, axis=0)
                out.append((e_re, e_im))
            return tuple(out)

        init = tuple((carry_ref[slot][:, :half], carry_ref[slot][:, half:]) for slot in range(2))
        final = lax.fori_loop(0, steps, body, init, unroll=4)
        for slot, (e_re, e_im) in enumerate(final):
            carry_ref[slot] = jnp.concatenate([e_re, e_im], axis=-1)

    x_f = gather_rows(xf_ref)
    sf_ref[...] = _dot(x_f, wsf_ref[...])
    x_b = gather_rows(xb_ref)
    sb_ref[...] = _dot(x_b, wsb_ref[...])
    scan()
    y_f = _dot(x_f, wi_ref[...]) + _dot(sf_ref[...].astype(BF16), wof_ref[...])
    scatter_rows(y_f, yf_ref)
    y_b = _dot(sb_ref[...].astype(BF16), wob_ref[...])
    scatter_rows(y_b, yb_ref)


def _s5(sx, s5w, bt, l, nb):
    lb = min(S5_ROWS * S5_CHUNK // nb, l)
    rb = lb // S5_CHUNK
    nl = l // lb
    sx4 = sx.reshape(SLABS, bt, l, LANES)
    w_in, w_intra, w_out, a = s5w
    wspec = lambda k: pl.BlockSpec((None, None, S5_K, S5_K), lambda s, g, i: (k, s, 0, 0))
    aspec = lambda k: pl.BlockSpec((None, None, 1, 2 * SLAB_STATE), lambda s, g, i: (k, s, 0, 0))
    fwd = pl.BlockSpec((None, nb, lb, LANES), lambda s, g, i: (s, g, i, 0))
    bwd = pl.BlockSpec((None, nb, lb, LANES), lambda s, g, i: (s, g, nl - 1 - i, 0))
    m = rb * nb
    yf, yb = pl.pallas_call(
        functools.partial(_s5_kernel, nb=nb, rb=rb),
        grid=(SLABS, bt // nb, nl),
        in_specs=[fwd, bwd, wspec(0), wspec(1), pl.BlockSpec((None, S5_K, S5_K), lambda s, g, i: (s, 0, 0)),
                  wspec(0), wspec(1), aspec(0), aspec(1)],
        out_specs=[fwd, bwd],
        out_shape=[jax.ShapeDtypeStruct(sx4.shape, F32)] * 2,
        scratch_shapes=[
            pltpu.VMEM((S5_CHUNK, m, LANES), F32),
            pltpu.VMEM((m, 2 * SLAB_STATE), F32),
            pltpu.VMEM((m, 2 * SLAB_STATE), F32),
            pltpu.VMEM((S5_CHUNK, m, LANES), F32),
            pltpu.VMEM((2, nb, 2 * SLAB_STATE), F32),
        ],
        compiler_params=_cparams(("parallel", "parallel", "arbitrary")),
        name="s5",
    )(sx4, sx4, w_in, w_in, w_intra, w_out, w_out, a, a)
    return yf.reshape(sx.shape), yb.reshape(sx.shape)


def _s5_weights(lam_re, lam_im, log_dt, b_re, b_im, c_re, c_im, d):
    hi = lax.Precision.HIGHEST
    tc, ng, ns, nc = S5_CHUNK, SLAB_GROUPS, SSM_STATE, SSM_GROUP
    direction = np.arange(2)[:, None]
    steps = np.arange(tc)
    dt = jnp.exp(log_dt)[..., None]
    mag = jnp.exp(lam_re * dt)
    a_re, a_im = mag * jnp.cos(lam_im * dt), mag * jnp.sin(lam_im * dt)
    den = lam_re * lam_re + lam_im * lam_im
    nr = a_re - 1.0
    k_re = (nr * lam_re + a_im * lam_im) / den
    k_im = (a_im * lam_re - nr * lam_im) / den
    tau = jnp.arange(tc + 1, dtype=F32)[None, :, None, None]
    pmag = jnp.exp((lam_re * dt)[:, None] * tau)
    p_re = pmag * jnp.cos((lam_im * dt)[:, None] * tau)
    p_im = pmag * jnp.sin((lam_im * dt)[:, None] * tau)

    group_of = lambda n, width: (jnp.arange(n, dtype=jnp.int32) // width) % ng
    own = lambda lane_group: jnp.arange(ng, dtype=jnp.int32)[:, None] == lane_group[None, :]

    sel = np.stack([tc - 1 - steps, steps])
    pq_re, pq_im = p_re[direction, sel], p_im[direction, sel]
    q_re = pq_re * k_re[:, None] - pq_im * k_im[:, None]
    q_im = pq_re * k_im[:, None] + pq_im * k_re[:, None]
    to_jlanes = lambda q: jnp.transpose(q.reshape(2, tc, SLABS, ng * ns), (0, 2, 1, 3))[:, :, :, None, None, :]
    to_clanes = lambda b: jnp.transpose(b.reshape(2, SLABS, ng, ns, nc), (0, 1, 4, 2, 3)).reshape(
        2, SLABS, 1, 1, nc, ng * ns)
    qr, qi = to_jlanes(q_re), to_jlanes(q_im)
    br, bi = to_clanes(b_re), to_clanes(b_im)
    keep = own(group_of(ng * ns, ns))[None, None, None, :, None, :]
    w_in = jnp.concatenate([jnp.where(keep, qr * br - qi * bi, 0.0), jnp.where(keep, qr * bi + qi * br, 0.0)],
                           axis=-1).astype(BF16).reshape(2, SLABS, S5_K, S5_K)

    sel = np.stack([steps + 1, tc - steps])
    lane = jnp.arange(S5_K, dtype=jnp.int32)
    spread = (jnp.arange(tc * ng, dtype=jnp.int32)[:, None]
              == ((lane // LANES) * ng + (lane % LANES) // nc)[None, :]).astype(F32)
    to_nrows = lambda p: jnp.dot(
        jnp.transpose(p[direction, sel].reshape(2, tc, SLABS, ng, ns), (0, 2, 4, 1, 3)).reshape(-1, tc * ng),
        spread, precision=hi).reshape(2, SLABS, ns, S5_K)
    to_dlanes = lambda c: jnp.tile(
        jnp.transpose(c.reshape(2, SLABS, ng, nc, ns), (0, 1, 4, 2, 3)).reshape(2, SLABS, ns, LANES), (1, 1, 1, tc))
    pr, pi = to_nrows(p_re), to_nrows(p_im)
    cr, ci = to_dlanes(c_re), to_dlanes(c_im)
    w = jnp.stack([cr * pr - ci * pi, -(cr * pi + ci * pr)], axis=2)
    keep = own(group_of(S5_K, nc))[None, None, None, :, None, :]
    w_out = jnp.where(keep, w[:, :, :, None], 0.0).astype(BF16).reshape(2, SLABS, S5_K, S5_K)

    cp_re = c_re[:, None] * p_re[:, :tc, :, None, :] - c_im[:, None] * p_im[:, :tc, :, None, :]
    cp_im = c_re[:, None] * p_im[:, :tc, :, None, :] + c_im[:, None] * p_re[:, :tc, :, None, :]
    bb_re = k_re[..., None] * b_re - k_im[..., None] * b_im
    bb_im = k_re[..., None] * b_im + k_im[..., None] * b_re
    kern = (jnp.einsum('klgdn,kgnc->kgcld', cp_re, bb_re, precision=hi)
            - jnp.einsum('klgdn,kgnc->kgcld', cp_im, bb_im, precision=hi))
    kern = jnp.transpose(kern.reshape(2, SLABS, ng, nc, tc, nc), (0, 1, 4, 3, 2, 5)).reshape(
        2, SLABS, tc, nc, LANES)
    chan = jnp.arange(nc, dtype=jnp.int32)
    skip = jnp.where(chan[:, None] == (jnp.arange(LANES, dtype=jnp.int32) % nc)[None, :],
                     d.reshape(SLABS, 1, LANES), 0.0)
    centre = (kern[0, :, 0] + kern[1, :, 0] + skip)[:, None]
    lags = jnp.concatenate([kern[1, :, :0:-1], centre, kern[0, :, 1:]], axis=1)
    rows = jnp.stack([jnp.concatenate([lags[:, tc - 1 + jp - j] for jp in range(tc)], axis=-1)
                      for j in range(tc)], axis=1)
    keep = own(group_of(S5_K, nc))[None, None, :, None, :]
    w_intra = jnp.where(keep, rows[:, :, None], 0.0).astype(BF16).reshape(SLABS, S5_K, S5_K)

    a = jnp.stack([p_re[:, tc], p_im[:, tc]], axis=1).reshape(2, 2, SLABS, SLAB_STATE)
    a = jnp.transpose(a, (0, 2, 1, 3)).reshape(2, SLABS, 1, 2 * SLAB_STATE)
    return w_in, w_intra, w_out, a


def _rel_bucket(rel):
    half = NUM_BUCKETS // 2
    max_exact = half // 2
    ret = (rel > 0).astype(jnp.int32) * half
    n = jnp.abs(rel)
    nf = jnp.maximum(n, 1).astype(jnp.float32)
    large = max_exact + (jnp.log(nf / max_exact) / math.log(MAX_DISTANCE / max_exact)
                         * (half - max_exact)).astype(jnp.int32)
    large = jnp.minimum(large, half - 1)
    return ret + jnp.where(n < max_exact, n, large)


def _bias_tiles(rel_bias, tile):
    assert tile >= MAX_DISTANCE
    qi = jnp.arange(tile, dtype=jnp.int32)[:, None]
    kj = jnp.arange(tile, dtype=jnp.int32)[None, :]
    offs = (jnp.arange(5, dtype=jnp.int32) - 2) * tile
    rel = offs[:, None, None] + kj[None] - qi[None]
    onehot = (_rel_bucket(rel)[None] == jnp.arange(NUM_BUCKETS, dtype=jnp.int32)[:, None, None, None])
    tab = jnp.einsum('nh,nabc->habc', rel_bias.astype(F32) * LOG2E, onehot.astype(F32),
                     precision=lax.Precision.HIGHEST)
    return tab


def _attn_kernel(lam_ref, q_ref, k_ref, v_ref, bias_ref, g_ref, o_ref,
                 sa_ref, sb_ref, ma_ref, mb_ref, *, tile, nk):
    hd = ATT_HEAD_DIM
    nt = (((1,), (1,)), ((), ()))
    lane_blocks = tile // LANES
    reps = ATT_V_DIM // LANES
    nq = nk
    assert nq % 2 == 0

    def tile_rows(i):
        return pl.ds(pl.multiple_of(i * tile, tile), tile)

    def step(i_qk, i_pv, w_bufs, r_bufs):
        if i_qk is not None:
            s_w, m_w = w_bufs
            q_next = [q_ref[tile_rows(i_qk), c * hd:(c + 1) * hd] for c in range(2)]
            mx = [jnp.full((tile, LANES), -jnp.inf, F32) for _ in range(2)]
        if i_pv is not None:
            s_r, m_r = r_bufs
            m_cur = [m_r[c] for c in range(2)]
            ls = [jnp.zeros((tile, LANES), F32) for _ in range(2)]
            acc = [None, None]
        for kj in range(nk):
            cols = slice(kj * tile, (kj + 1) * tile)
            if i_qk is not None:
                bias = bias_ref[jnp.clip(kj - i_qk, -2, 2) + 2]
                for c in range(2):
                    s = lax.dot_general(q_next[c], k_ref[cols, c * hd:(c + 1) * hd], nt,
                                        preferred_element_type=F32) + bias
                    s_w[c, :, cols] = s
                    for b in range(lane_blocks):
                        mx[c] = jnp.maximum(mx[c], s[:, b * LANES:(b + 1) * LANES])
            if i_pv is not None:
                vt = v_ref[cols, :]
                for c in range(2):
                    ps = []
                    for b in range(lane_blocks):
                        lo = kj * tile + b * LANES
                        p = jnp.exp2(s_r[c, :, lo:lo + LANES] - m_cur[c])
                        ls[c] = ls[c] + p
                        ps.append(p.astype(BF16))
                    d = _dot(jnp.concatenate(ps, axis=1), vt)
                    acc[c] = d if acc[c] is None else acc[c] + d
        if i_qk is not None:
            for c in range(2):
                m_w[c] = jnp.broadcast_to(jnp.max(mx[c], axis=1, keepdims=True), (tile, LANES))
        if i_pv is not None:
            inv0 = jnp.broadcast_to(1.0 / jnp.sum(ls[0], axis=1, keepdims=True), (tile, LANES))
            inv1 = jnp.broadcast_to(lam_ref[0] / jnp.sum(ls[1], axis=1, keepdims=True), (tile, LANES))
            o = (acc[0] * jnp.concatenate([inv0] * reps, axis=1)
                 - acc[1] * jnp.concatenate([inv1] * reps, axis=1))
            ms = jnp.mean(o * o, axis=-1, keepdims=True)
            o_ref[tile_rows(i_pv), :] = (o * lax.rsqrt(ms + EPS) * g_ref[...] * (1.0 - LAM_INIT)).astype(BF16)

    buf_a = (sa_ref, ma_ref)
    buf_b = (sb_ref, mb_ref)
    step(0, None, buf_a, None)

    def pair(j, carry):
        step(2 * j + 1, 2 * j, buf_b, buf_a)
        step(2 * j + 2, 2 * j + 1, buf_a, buf_b)
        return carry

    lax.fori_loop(0, nq // 2 - 1, pair, 0)
    step(nq - 1, nq - 2, buf_b, buf_a)
    step(None, nq - 1, None, buf_b)


def _attention(proj3, bias_tiles, subln_g, lam, tile):
    bt, l, _ = proj3.shape
    nk = l // tile
    blk = ATT_V_DIM
    seq = lambda c0: pl.BlockSpec((None, l, blk), lambda b, h: (b, 0, c0 // blk + h))
    return pl.pallas_call(
        functools.partial(_attn_kernel, tile=tile, nk=nk),
        grid=(bt, ATT_HEADS),
        in_specs=[
            pl.BlockSpec(memory_space=pltpu.SMEM),
            seq(COL_Q), seq(COL_K), seq(COL_V),
            pl.BlockSpec((None, 5, tile, tile), lambda b, h: (h, 0, 0, 0), pipeline_mode=pl.Buffered(1)),
            pl.BlockSpec((1, blk), lambda b, h: (0, 0)),
        ],
        out_specs=seq(0),
        out_shape=jax.ShapeDtypeStruct((bt, l, ATT_WIDTH), BF16),
        scratch_shapes=[
            pltpu.VMEM((2, tile, l), F32),
            pltpu.VMEM((2, tile, l), F32),
            pltpu.VMEM((2, tile, LANES), F32),
            pltpu.VMEM((2, tile, LANES), F32),
        ],
        compiler_params=_cparams(("parallel", "parallel")),
        name="attention",
    )(lam, proj3, proj3, proj3, bias_tiles, subln_g)


def _branches_kernel(yf_ref, yb_ref, z_ref, og_ref, gates_ref, gw_ref, gb_ref, ws_ref, wa_ref, o_ref):
    y = jnp.concatenate([yf_ref[s] + yb_ref[s] for s in range(SLABS)], axis=-1)
    y = jax.nn.gelu(y)
    y = y * _sigmoid(_dot(y.astype(BF16), gw_ref[...]) + gb_ref[...])
    sz = z_ref[:, :SSM_WIDTH].astype(F32)
    y = (y * (sz * _sigmoid(sz))).astype(BF16)
    y_s = _dot(y, ws_ref[...])
    az = z_ref[:, SSM_WIDTH:].astype(F32)
    o = (og_ref[...].astype(F32) * (az * _sigmoid(az))).astype(BF16)
    y_a = _dot(o, wa_ref[...])
    g_s = gates_ref[:, :D_MODEL].astype(F32)
    g_a = gates_ref[:, D_MODEL:].astype(F32)
    o_ref[...] = (_sigmoid(g_s) * y_s + _sigmoid(g_a) * y_a).astype(BF16)


def _branches(yf, yb, proj, og, glu_w, glu_b, w_s, w_a, tm):
    t = proj.shape[0]
    slab = pl.BlockSpec((SLABS, tm, LANES), lambda i: (0, i, 0))
    return pl.pallas_call(
        _branches_kernel,
        grid=(t // tm,),
        in_specs=[
            slab, slab,
            pl.BlockSpec((tm, SSM_WIDTH + ATT_WIDTH), lambda i: (i, COL_SZ // (SSM_WIDTH + ATT_WIDTH))),
            pl.BlockSpec((tm, ATT_WIDTH), lambda i: (i, 0)),
            pl.BlockSpec((tm, 2 * D_MODEL), lambda i: (i, COL_GATES // (2 * D_MODEL))),
            _const_spec((SSM_WIDTH, SSM_WIDTH)),
            _const_spec((1, SSM_WIDTH)),
            _const_spec((SSM_WIDTH, D_MODEL)),
            _const_spec((ATT_WIDTH, D_MODEL)),
        ],
        out_specs=pl.BlockSpec((tm, D_MODEL), lambda i: (i, 0)),
        out_shape=jax.ShapeDtypeStruct((t, D_MODEL), BF16),
        compiler_params=_cparams(("parallel",)),
        name="branches",
    )(yf, yb, proj, og, proj, glu_w, glu_b, w_s, w_a)


def _rms(x, g):
    ms = jnp.mean(x * x, axis=-1, keepdims=True)
    return x * lax.rsqrt(ms + EPS) * g


def _tail_kernel(m_ref, x_ref, p_ref, wo_ref, png_ref, pgw_ref, ppw_ref, fg_ref, o_ref):
    h = x_ref[...] + _dot(m_ref[...], wo_ref[...])
    gate = _sigmoid(_dot(_rms(h, png_ref[...]).astype(BF16), pgw_ref[...]))
    h = h + gate * _dot(p_ref[...].astype(BF16), ppw_ref[...])
    o_ref[...] = _rms(h, fg_ref[...])


def _tail(merged, x2d, p2d, w_out, ple_norm_g, ple_gate_w, ple_proj_w, final_g, tm):
    t = x2d.shape[0]
    row = lambda w: pl.BlockSpec((tm, w), lambda i: (i, 0))
    return pl.pallas_call(
        _tail_kernel,
        grid=(t // tm,),
        in_specs=[
            row(D_MODEL), row(D_MODEL), row(PLE_DIM),
            _const_spec((D_MODEL, D_MODEL)),
            _const_spec((1, D_MODEL)),
            _const_spec((D_MODEL, D_MODEL)),
            _const_spec((PLE_DIM, D_MODEL)),
            _const_spec((1, D_MODEL)),
        ],
        out_specs=row(D_MODEL),
        out_shape=jax.ShapeDtypeStruct((t, D_MODEL), F32),
        compiler_params=_cparams(("parallel",)),
        name="tail",
    )(merged, x2d, p2d, w_out, ple_norm_g, ple_gate_w, ple_proj_w, final_g)


def _prep_w_in(w):
    c = np.cumsum([0, SSM_WIDTH, SSM_WIDTH, ATT_QK_WIDTH, ATT_QK_WIDTH, ATT_WIDTH, ATT_WIDTH, D_MODEL, D_MODEL])
    part = lambda k: w[:, c[k]:c[k + 1]]
    return jnp.concatenate([part(6), part(7), part(1), part(5), part(2), part(3), part(4), part(0)],
                           axis=1).astype(BF16)


def _tiles(bt, l):
    t = bt * l
    tm_in = 1024 if t % 1024 == 0 else l
    tm = 512 if t % 512 == 0 else l
    nb = 8 if bt % 8 == 0 else (4 if bt % 4 == 0 else (2 if bt % 2 == 0 else 1))
    att = 256 if l % 256 == 0 else MAX_DISTANCE
    return tm_in, tm, nb, att


def _trunk(x, p, w):
    bt, l, _ = x.shape
    t = bt * l
    tm_in, tm, nb, att_tile = _tiles(bt, l)
    x2d = x.reshape(t, D_MODEL)
    proj, sx = _inproj(x2d, w['norm_g'], w['w_in'], tm_in)
    yf, yb = _s5(sx, w['s5'], bt, l, nb)
    og = _attention(proj.reshape(bt, l, PROJ_W), _bias_tiles(w['rel_bias'], att_tile), w['subln_g'],
                    w['lam'], att_tile)
    merged = _branches(yf, yb, proj, og.reshape(t, ATT_WIDTH), w['glu_w'], w['glu_b'],
                       w['w_branch_s'], w['w_branch_a'], tm)
    out = _tail(merged, x2d, p.reshape(t, PLE_DIM), w['w_out'], w['ple_norm_g'], w['ple_gate_w'],
                w['ple_proj_w'], w['final_g'], tm)
    return out.reshape(bt, l, D_MODEL)


def _prep_weights(rel_bias, norm_g, w_in, ssm_lambda_re, ssm_lambda_im, ssm_log_dt, ssm_b_re, ssm_b_im,
                  ssm_c_re, ssm_c_im, ssm_d, glu_w, glu_b, lam_q1, lam_k1, lam_q2, lam_k2, subln_g,
                  w_branch_s, w_branch_a, w_out, ple_norm_g, ple_gate_w, ple_proj_w, final_g):
    lam = (jnp.exp(jnp.sum(lam_q1[0] * lam_k1[0])) - jnp.exp(jnp.sum(lam_q2[0] * lam_k2[0])) + LAM_INIT)
    return {
        'rel_bias': rel_bias,
        'norm_g': norm_g[0].reshape(1, D_MODEL),
        'w_in': _prep_w_in(w_in[0]),
        's5': _s5_weights(ssm_lambda_re[0], ssm_lambda_im[0], ssm_log_dt[0], ssm_b_re[0], ssm_b_im[0],
                          ssm_c_re[0], ssm_c_im[0], ssm_d[0]),
        'glu_w': glu_w[0].astype(BF16),
        'glu_b': glu_b[0].reshape(1, SSM_WIDTH),
        'lam': lam.reshape(1).astype(F32),
        'subln_g': subln_g[0].reshape(1, ATT_V_DIM),
        'w_branch_s': w_branch_s[0].astype(BF16),
        'w_branch_a': w_branch_a[0].astype(BF16),
        'w_out': w_out[0].astype(BF16),
        'ple_norm_g': ple_norm_g[0].reshape(1, D_MODEL),
        'ple_gate_w': ple_gate_w[0].astype(BF16),
        'ple_proj_w': ple_proj_w[0].astype(BF16),
        'final_g': final_g.reshape(1, D_MODEL),
    }


def kernel(x_prompt, x_sample, p_prompt, p_sample, rel_bias, norm_g, w_in, ssm_lambda_re, ssm_lambda_im,
           ssm_log_dt, ssm_b_re, ssm_b_im, ssm_c_re, ssm_c_im, ssm_d, glu_w, glu_b, lam_q1, lam_k1, lam_q2,
           lam_k2, subln_g, w_branch_s, w_branch_a, w_out, ple_norm_g, ple_gate_w, ple_proj_w, final_g):
    w = _prep_weights(rel_bias, norm_g, w_in, ssm_lambda_re, ssm_lambda_im, ssm_log_dt, ssm_b_re, ssm_b_im,
                      ssm_c_re, ssm_c_im, ssm_d, glu_w, glu_b, lam_q1, lam_k1, lam_q2, lam_k2, subln_g,
                      w_branch_s, w_branch_a, w_out, ple_norm_g, ple_gate_w, ple_proj_w, final_g)
    return (_trunk(x_prompt, p_prompt[0], w), _trunk(x_sample, p_sample[0], w))
```

```python
import functools
import math

import jax
import jax.numpy as jnp
import numpy as np
from jax import lax
from jax.experimental import pallas as pl
from jax.experimental.pallas import tpu as pltpu

F32 = jnp.float32
BF16 = jnp.bfloat16

D_MODEL = 2048
SSM_WIDTH = 1024
SSM_GROUP = 16
SSM_GROUPS = SSM_WIDTH // SSM_GROUP
SSM_STATE = 64
ATT_HEADS = 4
ATT_HEAD_DIM = 128
ATT_V_DIM = 2 * ATT_HEAD_DIM
ATT_QK_WIDTH = ATT_HEADS * 2 * ATT_HEAD_DIM
ATT_WIDTH = ATT_HEADS * ATT_V_DIM
NUM_BUCKETS = 32
MAX_DISTANCE = 128
PLE_DIM = 256
EPS = 1e-6
LAM_INIT = 0.8 - 0.6 * math.exp(-0.3 * 0)
LOG2E = math.log2(math.e)
ATT_SCALE = ATT_HEAD_DIM ** -0.5 * LOG2E

LANES = 128
SUBLANES = 8
SLABS = SSM_WIDTH // LANES
SLAB_GROUPS = LANES // SSM_GROUP
S5_CHUNK = 8
SLAB_STATE = SLAB_GROUPS * SSM_STATE
S5_K = S5_CHUNK * LANES
S5_ROWS = 512

IN_WIDTH = 2 * SSM_WIDTH + 2 * ATT_QK_WIDTH + 2 * ATT_WIDTH + 2 * D_MODEL
PROJ_W = IN_WIDTH - SSM_WIDTH
COL_GATES = 0
COL_SZ = 4096
COL_AZ = 5120
COL_Q = 6144
COL_K = 7168
COL_V = 8192
IN_TN = 1024
Q_TILE_INDEX = COL_Q // IN_TN
VMEM_LIMIT = 56 * 1024 * 1024


def _cparams(sem):
    return pltpu.CompilerParams(dimension_semantics=sem, vmem_limit_bytes=VMEM_LIMIT)


def _dot(a, b):
    return jnp.dot(a, b, preferred_element_type=F32)


def _sigmoid(x):
    return 0.5 * jnp.tanh(0.5 * x) + 0.5


def _const_spec(shape):
    zeros = (0,) * len(shape)
    return pl.BlockSpec(shape, lambda *_: zeros, pipeline_mode=pl.Buffered(1))


def _inproj_kernel(x_ref, g_ref, w_ref, proj_ref, sx_ref, hn_ref):
    j = pl.program_id(1)
    last = pl.num_programs(1) - 1

    @pl.when(j == 0)
    def _():
        x = x_ref[...]
        ms = jnp.mean(x * x, axis=-1, keepdims=True)
        hn_ref[...] = (x * lax.rsqrt(ms + EPS) * g_ref[...]).astype(BF16)

    acc = _dot(hn_ref[...], w_ref[...])

    @pl.when(j < last)
    def _():
        scale = jnp.where(j == Q_TILE_INDEX, ATT_SCALE, 1.0).astype(F32)
        proj_ref[...] = (acc * scale).astype(BF16)

    @pl.when(j == last)
    def _():
        for s in range(SLABS):
            sx_ref[s] = acc[:, s * LANES:(s + 1) * LANES]


def _inproj(x2d, norm_g, w_in, tm):
    t = x2d.shape[0]
    nj = w_in.shape[1] // IN_TN
    return pl.pallas_call(
        _inproj_kernel,
        grid=(t // tm, nj),
        in_specs=[
            pl.BlockSpec((tm, D_MODEL), lambda i, j: (i, 0)),
            pl.BlockSpec((1, D_MODEL), lambda i, j: (0, 0)),
            pl.BlockSpec((D_MODEL, IN_TN), lambda i, j: (0, j)),
        ],
        out_specs=[
            pl.BlockSpec((tm, IN_TN), lambda i, j: (i, jnp.minimum(j, nj - 2))),
            pl.BlockSpec((SLABS, tm, LANES), lambda i, j: (0, i, 0)),
        ],
        out_shape=[
            jax.ShapeDtypeStruct((t, PROJ_W), BF16),
            jax.ShapeDtypeStruct((SLABS, t, LANES), F32),
        ],
        scratch_shapes=[pltpu.VMEM((tm, D_MODEL), BF16)],
        compiler_params=_cparams(("parallel", "arbitrary")),
        name="inproj",
    )(x2d, norm_g, w_in)


def _s5_kernel(xf_ref, xb_ref, wsf_ref, wsb_ref, wi_ref, wof_ref, wob_ref, af_ref, ab_ref,
               yf_ref, yb_ref, xs_ref, sf_ref, sb_ref, ys_ref, carry_ref, *, nb, rb):
    i = pl.program_id(2)
    half = SLAB_STATE

    @pl.when(i == 0)
    def _():
        carry_ref[...] = jnp.zeros_like(carry_ref)

    def gather_rows(x_ref):
        for j in range(S5_CHUNK):
            for b in range(nb):
                xs_ref[j, pl.ds(b, rb, stride=nb), :] = x_ref[b, pl.ds(j, rb, stride=S5_CHUNK), :]
        return jnp.concatenate([xs_ref[j] for j in range(S5_CHUNK)], axis=-1).astype(BF16)

    def scatter_rows(y, y_ref):
        for j in range(S5_CHUNK):
            ys_ref[j] = y[:, j * LANES:(j + 1) * LANES]
        for j in range(S5_CHUNK):
            for b in range(nb):
                y_ref[b, pl.ds(j, rb, stride=S5_CHUNK), :] = ys_ref[j, pl.ds(b, rb, stride=nb), :]

    def scan():
        per = max(1, SUBLANES // nb)
        rows = per * nb
        steps = rb // per
        dirs = ((sf_ref, af_ref, False), (sb_ref, ab_ref, True))
        poles = [(jnp.broadcast_to(a_ref[:, :half], (nb, half)), jnp.broadcast_to(a_ref[:, half:], (nb, half)))
                 for _, a_ref, _ in dirs]

        def body(step, carry):
            out = []
            for (s_ref, _, reverse), (a_re, a_im), (e_re, e_im) in zip(dirs, poles, carry):
                row = ((steps - 1 - step) if reverse else step) * rows
                s = s_ref[row:row + rows, :]
                entering = [None] * per
                for q in (range(per - 1, -1, -1) if reverse else range(per)):
                    sq = s[q * nb:(q + 1) * nb]
                    entering[q] = jnp.concatenate([e_re, e_im], axis=-1)
                    e_re, e_im = (a_re * e_re - a_im * e_im + sq[:, :half],
                                  a_re * e_im + a_im * e_re + sq[:, half:])
                s_ref[row:row + rows, :] = jnp.concatenate(entering, axis=0)
                out.append((e_re, e_im))
            return tuple(out)

        carry = tuple((carry_ref[slot][:, :half], carry_ref[slot][:, half:]) for slot in range(2))
        for step in range(steps):
            carry = body(step, carry)
        for slot, (e_re, e_im) in enumerate(carry):
            carry_ref[slot] = jnp.concatenate([e_re, e_im], axis=-1)

    x_f = gather_rows(xf_ref)
    sf_ref[...] = _dot(x_f, wsf_ref[...])
    x_b = gather_rows(xb_ref)
    sb_ref[...] = _dot(x_b, wsb_ref[...])
    y_intra = _dot(x_f, wi_ref[...])
    scan()
    y_f = y_intra + _dot(sf_ref[...].astype(BF16), wof_ref[...])
    scatter_rows(y_f, yf_ref)
    y_b = _dot(sb_ref[...].astype(BF16), wob_ref[...])
    scatter_rows(y_b, yb_ref)


def _s5(sx, s5w, bt, l, nb):
    lb = min(S5_ROWS * S5_CHUNK // nb, l)
    rb = lb // S5_CHUNK
    nl = l // lb
    sx4 = sx.reshape(SLABS, bt, l, LANES)
    w_in, w_intra, w_out, a = s5w
    wspec = lambda k: pl.BlockSpec((None, None, S5_K, S5_K), lambda s, g, i: (k, s, 0, 0))
    aspec = lambda k: pl.BlockSpec((None, None, 1, 2 * SLAB_STATE), lambda s, g, i: (k, s, 0, 0))
    fwd = pl.BlockSpec((None, nb, lb, LANES), lambda s, g, i: (s, g, i, 0))
    bwd = pl.BlockSpec((None, nb, lb, LANES), lambda s, g, i: (s, g, nl - 1 - i, 0))
    m = rb * nb
    yf, yb = pl.pallas_call(
        functools.partial(_s5_kernel, nb=nb, rb=rb),
        grid=(SLABS, bt // nb, nl),
        in_specs=[fwd, bwd, wspec(0), wspec(1), pl.BlockSpec((None, S5_K, S5_K), lambda s, g, i: (s, 0, 0)),
                  wspec(0), wspec(1), aspec(0), aspec(1)],
        out_specs=[fwd, bwd],
        out_shape=[jax.ShapeDtypeStruct(sx4.shape, F32)] * 2,
        scratch_shapes=[
            pltpu.VMEM((S5_CHUNK, m, LANES), F32),
            pltpu.VMEM((m, 2 * SLAB_STATE), F32),
            pltpu.VMEM((m, 2 * SLAB_STATE), F32),
            pltpu.VMEM((S5_CHUNK, m, LANES), F32),
            pltpu.VMEM((2, nb, 2 * SLAB_STATE), F32),
        ],
        compiler_params=_cparams(("parallel", "parallel", "arbitrary")),
        name="s5",
    )(sx4, sx4, w_in, w_in, w_intra, w_out, w_out, a, a)
    return yf.reshape(sx.shape), yb.reshape(sx.shape)


def _s5_weights(lam_re, lam_im, log_dt, b_re, b_im, c_re, c_im, d):
    hi = lax.Precision.HIGHEST
    tc, ng, ns, nc = S5_CHUNK, SLAB_GROUPS, SSM_STATE, SSM_GROUP
    direction = np.arange(2)[:, None]
    steps = np.arange(tc)
    dt = jnp.exp(log_dt)[..., None]
    mag = jnp.exp(lam_re * dt)
    a_re, a_im = mag * jnp.cos(lam_im * dt), mag * jnp.sin(lam_im * dt)
    den = lam_re * lam_re + lam_im * lam_im
    nr = a_re - 1.0
    k_re = (nr * lam_re + a_im * lam_im) / den
    k_im = (a_im * lam_re - nr * lam_im) / den
    tau = jnp.arange(tc + 1, dtype=F32)[None, :, None, None]
    pmag = jnp.exp((lam_re * dt)[:, None] * tau)
    p_re = pmag * jnp.cos((lam_im * dt)[:, None] * tau)
    p_im = pmag * jnp.sin((lam_im * dt)[:, None] * tau)

    group_of = lambda n, width: (jnp.arange(n, dtype=jnp.int32) // width) % ng
    own = lambda lane_group: jnp.arange(ng, dtype=jnp.int32)[:, None] == lane_group[None, :]

    sel = np.stack([tc - 1 - steps, steps])
    pq_re, pq_im = p_re[direction, sel], p_im[direction, sel]
    q_re = pq_re * k_re[:, None] - pq_im * k_im[:, None]
    q_im = pq_re * k_im[:, None] + pq_im * k_re[:, None]
    to_jlanes = lambda q: jnp.transpose(q.reshape(2, tc, SLABS, ng * ns), (0, 2, 1, 3))[:, :, :, None, None, :]
    to_clanes = lambda b: jnp.transpose(b.reshape(2, SLABS, ng, ns, nc), (0, 1, 4, 2, 3)).reshape(
        2, SLABS, 1, 1, nc, ng * ns)
    qr, qi = to_jlanes(q_re), to_jlanes(q_im)
    br, bi = to_clanes(b_re), to_clanes(b_im)
    keep = own(group_of(ng * ns, ns))[None, None, None, :, None, :]
    w_in = jnp.concatenate([jnp.where(keep, qr * br - qi * bi, 0.0), jnp.where(keep, qr * bi + qi * br, 0.0)],
                           axis=-1).astype(BF16).reshape(2, SLABS, S5_K, S5_K)

    sel = np.stack([steps + 1, tc - steps])
    lane = jnp.arange(S5_K, dtype=jnp.int32)
    spread = (jnp.arange(tc * ng, dtype=jnp.int32)[:, None]
              == ((lane // LANES) * ng + (lane % LANES) // nc)[None, :]).astype(F32)
    to_nrows = lambda p: jnp.dot(
        jnp.transpose(p[direction, sel].reshape(2, tc, SLABS, ng, ns), (0, 2, 4, 1, 3)).reshape(-1, tc * ng),
        spread, precision=hi).reshape(2, SLABS, ns, S5_K)
    to_dlanes = lambda c: jnp.tile(
        jnp.transpose(c.reshape(2, SLABS, ng, nc, ns), (0, 1, 4, 2, 3)).reshape(2, SLABS, ns, LANES), (1, 1, 1, tc))
    pr, pi = to_nrows(p_re), to_nrows(p_im)
    cr, ci = to_dlanes(c_re), to_dlanes(c_im)
    w = jnp.stack([cr * pr - ci * pi, -(cr * pi + ci * pr)], axis=2)
    keep = own(group_of(S5_K, nc))[None, None, None, :, None, :]
    w_out = jnp.where(keep, w[:, :, :, None], 0.0).astype(BF16).reshape(2, SLABS, S5_K, S5_K)

    cp_re = c_re[:, None] * p_re[:, :tc, :, None, :] - c_im[:, None] * p_im[:, :tc, :, None, :]
    cp_im = c_re[:, None] * p_im[:, :tc, :, None, :] + c_im[:, None] * p_re[:, :tc, :, None, :]
    bb_re = k_re[..., None] * b_re - k_im[..., None] * b_im
    bb_im = k_re[..., None] * b_im + k_im[..., None] * b_re
    kern = jnp.einsum('klgdn,kgnc->kgcld', jnp.concatenate([cp_re, -cp_im], axis=-1),
                      jnp.concatenate([bb_re, bb_im], axis=-2), precision=lax.Precision.HIGH)
    kern = jnp.transpose(kern.reshape(2, SLABS, ng, nc, tc, nc), (0, 1, 4, 3, 2, 5)).reshape(
        2, SLABS, tc, nc, LANES)
    chan = jnp.arange(nc, dtype=jnp.int32)
    skip = jnp.where(chan[:, None] == (jnp.arange(LANES, dtype=jnp.int32) % nc)[None, :],
                     d.reshape(SLABS, 1, LANES), 0.0)
    centre = (kern[0, :, 0] + kern[1, :, 0] + skip)[:, None]
    lags = jnp.concatenate([kern[1, :, :0:-1], centre, kern[0, :, 1:]], axis=1)
    rows = jnp.stack([jnp.concatenate([lags[:, tc - 1 + jp - j] for jp in range(tc)], axis=-1)
                      for j in range(tc)], axis=1)
    keep = own(group_of(S5_K, nc))[None, None, :, None, :]
    w_intra = jnp.where(keep, rows[:, :, None], 0.0).astype(BF16).reshape(SLABS, S5_K, S5_K)

    a = jnp.stack([p_re[:, tc], p_im[:, tc]], axis=1).reshape(2, 2, SLABS, SLAB_STATE)
    a = jnp.transpose(a, (0, 2, 1, 3)).reshape(2, SLABS, 1, 2 * SLAB_STATE)
    return w_in, w_intra, w_out, a


def _rel_bucket(rel):
    half = NUM_BUCKETS // 2
    max_exact = half // 2
    ret = (rel > 0).astype(jnp.int32) * half
    n = jnp.abs(rel)
    nf = jnp.maximum(n, 1).astype(jnp.float32)
    large = max_exact + (jnp.log(nf / max_exact) / math.log(MAX_DISTANCE / max_exact)
                         * (half - max_exact)).astype(jnp.int32)
    large = jnp.minimum(large, half - 1)
    return ret + jnp.where(n < max_exact, n, large)


def _bias_tiles(rel_bias, tile):
    assert tile >= MAX_DISTANCE
    qi = jnp.arange(tile, dtype=jnp.int32)[:, None]
    kj = jnp.arange(tile, dtype=jnp.int32)[None, :]
    offs = (jnp.arange(5, dtype=jnp.int32) - 2) * tile
    rel = offs[:, None, None] + kj[None] - qi[None]
    onehot = (_rel_bucket(rel)[None] == jnp.arange(NUM_BUCKETS, dtype=jnp.int32)[:, None, None, None])
    tab = jnp.einsum('nh,nabc->habc', rel_bias.astype(F32) * LOG2E, onehot.astype(F32),
                     precision=lax.Precision.HIGHEST)
    return tab


def _attn_kernel(lam_ref, q_ref, k_ref, v_ref, bias_ref, g_ref, o_ref,
                 sa_ref, sb_ref, ma_ref, mb_ref, *, tile, nk):
    hd = ATT_HEAD_DIM
    nt = (((1,), (1,)), ((), ()))
    lane_blocks = tile // LANES
    reps = ATT_V_DIM // LANES
    nq = nk
    assert nq % 2 == 0

    def tile_rows(i):
        return pl.ds(pl.multiple_of(i * tile, tile), tile)

    def step(i_qk, i_pv, w_bufs, r_bufs):
        if i_qk is not None:
            s_w, m_w = w_bufs
            q_next = [q_ref[tile_rows(i_qk), c * hd:(c + 1) * hd] for c in range(2)]
            mx = [jnp.full((tile, LANES), -jnp.inf, F32) for _ in range(2)]
        if i_pv is not None:
            s_r, m_r = r_bufs
            m_cur = [m_r[c] for c in range(2)]
            ls = [jnp.zeros((tile, LANES), F32) for _ in range(2)]
            acc = [None, None]
        for kj in range(nk):
            cols = slice(kj * tile, (kj + 1) * tile)
            if i_qk is not None:
                bias = bias_ref[jnp.clip(kj - i_qk, -2, 2) + 2]
                for c in range(2):
                    s = lax.dot_general(q_next[c], k_ref[cols, c * hd:(c + 1) * hd], nt,
                                        preferred_element_type=F32) + bias
                    s_w[c, :, cols] = s
                    mx[c] = jnp.maximum(mx[c], functools.reduce(
                        jnp.maximum, [s[:, b * LANES:(b + 1) * LANES] for b in range(lane_blocks)]))
            if i_pv is not None:
                vt = v_ref[cols, :]
                for c in range(2):
                    ps = []
                    for b in range(lane_blocks):
                        lo = kj * tile + b * LANES
                        ps.append(jnp.exp2(s_r[c, :, lo:lo + LANES] - m_cur[c]))
                    ls[c] = ls[c] + functools.reduce(jnp.add, ps)
                    d = _dot(jnp.concatenate([p.astype(BF16) for p in ps], axis=1), vt)
                    acc[c] = d if acc[c] is None else acc[c] + d
        if i_qk is not None:
            for c in range(2):
                m_w[c] = jnp.broadcast_to(jnp.max(mx[c], axis=1, keepdims=True), (tile, LANES))
        if i_pv is not None:
            inv0 = jnp.broadcast_to(1.0 / jnp.sum(ls[0], axis=1, keepdims=True), (tile, LANES))
            inv1 = jnp.broadcast_to(lam_ref[0] / jnp.sum(ls[1], axis=1, keepdims=True), (tile, LANES))
            o = (acc[0] * jnp.concatenate([inv0] * reps, axis=1)
                 - acc[1] * jnp.concatenate([inv1] * reps, axis=1))
            ms = jnp.mean(o * o, axis=-1, keepdims=True)
            o_ref[tile_rows(i_pv), :] = (o * lax.rsqrt(ms + EPS) * g_ref[...] * (1.0 - LAM_INIT)).astype(BF16)

    buf_a = (sa_ref, ma_ref)
    buf_b = (sb_ref, mb_ref)
    step(0, None, buf_a, None)

    def pair(j, carry):
        step(2 * j + 1, 2 * j, buf_b, buf_a)
        step(2 * j + 2, 2 * j + 1, buf_a, buf_b)
        return carry

    lax.fori_loop(0, nq // 2 - 1, pair, 0)
    step(nq - 1, nq - 2, buf_b, buf_a)
    step(None, nq - 1, None, buf_b)


def _attention(proj3, bias_tiles, subln_g, lam, tile):
    bt, l, _ = proj3.shape
    nk = l // tile
    blk = ATT_V_DIM
    seq = lambda c0: pl.BlockSpec((None, l, blk), lambda b, h: (b, 0, c0 // blk + h))
    return pl.pallas_call(
        functools.partial(_attn_kernel, tile=tile, nk=nk),
        grid=(bt, ATT_HEADS),
        in_specs=[
            pl.BlockSpec(memory_space=pltpu.SMEM),
            seq(COL_Q), seq(COL_K), seq(COL_V),
            pl.BlockSpec((None, 5, tile, tile), lambda b, h: (h, 0, 0, 0), pipeline_mode=pl.Buffered(1)),
            pl.BlockSpec((1, blk), lambda b, h: (0, 0)),
        ],
        out_specs=seq(0),
        out_shape=jax.ShapeDtypeStruct((bt, l, ATT_WIDTH), BF16),
        scratch_shapes=[
            pltpu.VMEM((2, tile, l), F32),
            pltpu.VMEM((2, tile, l), F32),
            pltpu.VMEM((2, tile, LANES), F32),
            pltpu.VMEM((2, tile, LANES), F32),
        ],
        compiler_params=_cparams(("parallel", "parallel")),
        name="attention",
    )(lam, proj3, proj3, proj3, bias_tiles, subln_g)


def _branches_kernel(yf_ref, yb_ref, z_ref, og_ref, gates_ref, gw_ref, gb_ref, ws_ref, wa_ref, o_ref):
    y = jnp.concatenate([yf_ref[s] + yb_ref[s] for s in range(SLABS)], axis=-1)
    y = jax.nn.gelu(y)
    y = y * _sigmoid(_dot(y.astype(BF16), gw_ref[...]) + gb_ref[...])
    sz = z_ref[:, :SSM_WIDTH].astype(F32)
    y = (y * (sz * _sigmoid(sz))).astype(BF16)
    y_s = _dot(y, ws_ref[...])
    az = z_ref[:, SSM_WIDTH:].astype(F32)
    o = (og_ref[...].astype(F32) * (az * _sigmoid(az))).astype(BF16)
    y_a = _dot(o, wa_ref[...])
    g_s = gates_ref[:, :D_MODEL].astype(F32)
    g_a = gates_ref[:, D_MODEL:].astype(F32)
    o_ref[...] = (_sigmoid(g_s) * y_s + _sigmoid(g_a) * y_a).astype(BF16)


def _branches(yf, yb, proj, og, glu_w, glu_b, w_s, w_a, tm):
    t = proj.shape[0]
    slab = pl.BlockSpec((SLABS, tm, LANES), lambda i: (0, i, 0))
    return pl.pallas_call(
        _branches_kernel,
        grid=(t // tm,),
        in_specs=[
            slab, slab,
            pl.BlockSpec((tm, SSM_WIDTH + ATT_WIDTH), lambda i: (i, COL_SZ // (SSM_WIDTH + ATT_WIDTH))),
            pl.BlockSpec((tm, ATT_WIDTH), lambda i: (i, 0)),
            pl.BlockSpec((tm, 2 * D_MODEL), lambda i: (i, COL_GATES // (2 * D_MODEL))),
            _const_spec((SSM_WIDTH, SSM_WIDTH)),
            _const_spec((1, SSM_WIDTH)),
            _const_spec((SSM_WIDTH, D_MODEL)),
            _const_spec((ATT_WIDTH, D_MODEL)),
        ],
        out_specs=pl.BlockSpec((tm, D_MODEL), lambda i: (i, 0)),
        out_shape=jax.ShapeDtypeStruct((t, D_MODEL), BF16),
        compiler_params=_cparams(("parallel",)),
        name="branches",
    )(yf, yb, proj, og, proj, glu_w, glu_b, w_s, w_a)


def _rms(x, g):
    ms = jnp.mean(x * x, axis=-1, keepdims=True)
    return x * lax.rsqrt(ms + EPS) * g


def _tail_kernel(m_ref, x_ref, p_ref, wo_ref, png_ref, pgw_ref, ppw_ref, fg_ref, o_ref):
    h = x_ref[...] + _dot(m_ref[...], wo_ref[...])
    gate = _sigmoid(_dot(_rms(h, png_ref[...]).astype(BF16), pgw_ref[...]))
    h = h + gate * _dot(p_ref[...].astype(BF16), ppw_ref[...])
    o_ref[...] = _rms(h, fg_ref[...])


def _tail(merged, x2d, p2d, w_out, ple_norm_g, ple_gate_w, ple_proj_w, final_g, tm):
    t = x2d.shape[0]
    row = lambda w: pl.BlockSpec((tm, w), lambda i: (i, 0))
    return pl.pallas_call(
        _tail_kernel,
        grid=(t // tm,),
        in_specs=[
            row(D_MODEL), row(D_MODEL), row(PLE_DIM),
            _const_spec((D_MODEL, D_MODEL)),
            _const_spec((1, D_MODEL)),
            _const_spec((D_MODEL, D_MODEL)),
            _const_spec((PLE_DIM, D_MODEL)),
            _const_spec((1, D_MODEL)),
        ],
        out_specs=row(D_MODEL),
        out_shape=jax.ShapeDtypeStruct((t, D_MODEL), F32),
        compiler_params=_cparams(("parallel",)),
        name="tail",
    )(merged, x2d, p2d, w_out, ple_norm_g, ple_gate_w, ple_proj_w, final_g)


def _prep_w_in(w):
    c = np.cumsum([0, SSM_WIDTH, SSM_WIDTH, ATT_QK_WIDTH, ATT_QK_WIDTH, ATT_WIDTH, ATT_WIDTH, D_MODEL, D_MODEL])
    part = lambda k: w[:, c[k]:c[k + 1]]
    return jnp.concatenate([part(6), part(7), part(1), part(5), part(2), part(3), part(4), part(0)],
                           axis=1).astype(BF16)


def _tiles(bt, l):
    t = bt * l
    tm_in = 1024 if t % 1024 == 0 else l
    tm = 512 if t % 512 == 0 else l
    nb = 8 if bt % 8 == 0 else (4 if bt % 4 == 0 else (2 if bt % 2 == 0 else 1))
    att = 256 if l % 256 == 0 else MAX_DISTANCE
    return tm_in, tm, nb, att


def _trunk(x, p, w):
    bt, l, _ = x.shape
    t = bt * l
    tm_in, tm, nb, att_tile = _tiles(bt, l)
    x2d = x.reshape(t, D_MODEL)
    proj, sx = _inproj(x2d, w['norm_g'], w['w_in'], tm_in)
    yf, yb = _s5(sx, w['s5'], bt, l, nb)
    og = _attention(proj.reshape(bt, l, PROJ_W), _bias_tiles(w['rel_bias'], att_tile), w['subln_g'],
                    w['lam'], att_tile)
    merged = _branches(yf, yb, proj, og.reshape(t, ATT_WIDTH), w['glu_w'], w['glu_b'],
                       w['w_branch_s'], w['w_branch_a'], tm)
    out = _tail(merged, x2d, p.reshape(t, PLE_DIM), w['w_out'], w['ple_norm_g'], w['ple_gate_w'],
                w['ple_proj_w'], w['final_g'], tm)
    return out.reshape(bt, l, D_MODEL)


def _prep_weights(rel_bias, norm_g, w_in, ssm_lambda_re, ssm_lambda_im, ssm_log_dt, ssm_b_re, ssm_b_im,
                  ssm_c_re, ssm_c_im, ssm_d, glu_w, glu_b, lam_q1, lam_k1, lam_q2, lam_k2, subln_g,
                  w_branch_s, w_branch_a, w_out, ple_norm_g, ple_gate_w, ple_proj_w, final_g):
    lam = (jnp.exp(jnp.sum(lam_q1[0] * lam_k1[0])) - jnp.exp(jnp.sum(lam_q2[0] * lam_k2[0])) + LAM_INIT)
    return {
        'rel_bias': rel_bias,
        'norm_g': norm_g[0].reshape(1, D_MODEL),
        'w_in': _prep_w_in(w_in[0]),
        's5': _s5_weights(ssm_lambda_re[0], ssm_lambda_im[0], ssm_log_dt[0], ssm_b_re[0], ssm_b_im[0],
                          ssm_c_re[0], ssm_c_im[0], ssm_d[0]),
        'glu_w': glu_w[0].astype(BF16),
        'glu_b': glu_b[0].reshape(1, SSM_WIDTH),
        'lam': lam.reshape(1).astype(F32),
        'subln_g': subln_g[0].reshape(1, ATT_V_DIM),
        'w_branch_s': w_branch_s[0].astype(BF16),
        'w_branch_a': w_branch_a[0].astype(BF16),
        'w_out': w_out[0].astype(BF16),
        'ple_norm_g': ple_norm_g[0].reshape(1, D_MODEL),
        'ple_gate_w': ple_gate_w[0].astype(BF16),
        'ple_proj_w': ple_proj_w[0].astype(BF16),
        'final_g': final_g.reshape(1, D_MODEL),
    }


def kernel(x_prompt, x_sample, p_prompt, p_sample, rel_bias, norm_g, w_in, ssm_lambda_re, ssm_lambda_im,
           ssm_log_dt, ssm_b_re, ssm_b_im, ssm_c_re, ssm_c_im, ssm_d, glu_w, glu_b, lam_q1, lam_k1, lam_q2,
           lam_k2, subln_g, w_branch_s, w_branch_a, w_out, ple_norm_g, ple_gate_w, ple_proj_w, final_g):
    w = _prep_weights(rel_bias, norm_g, w_in, ssm_lambda_re, ssm_lambda_im, ssm_log_dt, ssm_b_re, ssm_b_im,
                      ssm_c_re, ssm_c_im, ssm_d, glu_w, glu_b, lam_q1, lam_k1, lam_q2, lam_k2, subln_g,
                      w_branch_s, w_branch_a, w_out, ple_norm_g, ple_gate_w, ple_proj_w, final_g)
    return (_trunk(x_prompt, p_prompt[0], w), _trunk(x_sample, p_sample[0], w))
```

```python
import functools
import math

import jax
import jax.numpy as jnp
import numpy as np
from jax import lax
from jax.experimental import pallas as pl
from jax.experimental.pallas import tpu as pltpu

F32 = jnp.float32
BF16 = jnp.bfloat16

D_MODEL = 2048
SSM_WIDTH = 1024
SSM_GROUP = 16
SSM_GROUPS = SSM_WIDTH // SSM_GROUP
SSM_STATE = 64
ATT_HEADS = 4
ATT_HEAD_DIM = 128
ATT_V_DIM = 2 * ATT_HEAD_DIM
ATT_QK_WIDTH = ATT_HEADS * 2 * ATT_HEAD_DIM
ATT_WIDTH = ATT_HEADS * ATT_V_DIM
NUM_BUCKETS = 32
MAX_DISTANCE = 128
PLE_DIM = 256
EPS = 1e-6
LAM_INIT = 0.8 - 0.6 * math.exp(-0.3 * 0)
LOG2E = math.log2(math.e)
ATT_SCALE = ATT_HEAD_DIM ** -0.5 * LOG2E

LANES = 128
SUBLANES = 8
SLABS = SSM_WIDTH // LANES
SLAB_GROUPS = LANES // SSM_GROUP
S5_CHUNK = 8
SLAB_STATE = SLAB_GROUPS * SSM_STATE
S5_K = S5_CHUNK * LANES
S5_ROWS = 512

IN_WIDTH = 2 * SSM_WIDTH + 2 * ATT_QK_WIDTH + 2 * ATT_WIDTH + 2 * D_MODEL
PROJ_W = IN_WIDTH
COL_GATES = 0
COL_SZ = COL_GATES + 2 * D_MODEL
COL_AZ = COL_SZ + SSM_WIDTH
COL_Q = COL_AZ + ATT_WIDTH
COL_K = COL_Q + ATT_QK_WIDTH
COL_V = COL_K + ATT_QK_WIDTH
COL_SX = COL_V + ATT_WIDTH
IN_TN = 2048
VMEM_LIMIT = 56 * 1024 * 1024


def _cparams(sem):
    return pltpu.CompilerParams(dimension_semantics=sem, vmem_limit_bytes=VMEM_LIMIT)


def _dot(a, b):
    return jnp.dot(a, b, preferred_element_type=F32)


def _sigmoid(x):
    return 0.5 * jnp.tanh(0.5 * x) + 0.5


def _const_spec(shape):
    zeros = (0,) * len(shape)
    return pl.BlockSpec(shape, lambda *_: zeros, pipeline_mode=pl.Buffered(1))


def _inproj_kernel(x_hbm, g_ref, w_ref, proj_ref, sx_ref, xbuf_ref, hn_ref, sem, *, tm):
    i = pl.program_id(0)
    j = pl.program_id(1)

    def x_copy(tile):
        return pltpu.make_async_copy(x_hbm.at[pl.ds(tile * tm, tm)], xbuf_ref, sem)

    @pl.when(j == 0)
    def _():
        @pl.when(i == 0)
        def _():
            x_copy(0).start()

        x_copy(i).wait()
        x = xbuf_ref[...]
        ms = jnp.mean(x * x, axis=-1, keepdims=True)
        hn_ref[...] = (x * lax.rsqrt(ms + EPS) * g_ref[...]).astype(BF16)

        @pl.when(i + 1 < pl.num_programs(0))
        def _():
            x_copy(i + 1).start()

    acc = _dot(hn_ref[...], w_ref[...])
    col = lax.broadcasted_iota(jnp.int32, (1, IN_TN), 1)
    scale = jnp.where((j == COL_Q // IN_TN) & (col < ATT_QK_WIDTH), ATT_SCALE, 1.0).astype(F32)
    proj_ref[...] = (acc * scale).astype(BF16)

    @pl.when(j == COL_SX // IN_TN)
    def _():
        base = COL_SX % IN_TN
        for s in range(SLABS):
            sx_ref[s] = acc[:, base + s * LANES:base + (s + 1) * LANES]


def _inproj(x2d, norm_g, w_in, tm):
    t = x2d.shape[0]
    return pl.pallas_call(
        functools.partial(_inproj_kernel, tm=tm),
        grid=(t // tm, IN_WIDTH // IN_TN),
        in_specs=[
            pl.BlockSpec(memory_space=pl.ANY),
            pl.BlockSpec((1, D_MODEL), lambda i, j: (0, 0)),
            pl.BlockSpec((D_MODEL, IN_TN), lambda i, j: (0, j)),
        ],
        out_specs=[
            pl.BlockSpec((tm, IN_TN), lambda i, j: (i, j)),
            pl.BlockSpec((SLABS, tm, LANES), lambda i, j: (0, i, 0)),
        ],
        out_shape=[
            jax.ShapeDtypeStruct((t, PROJ_W), BF16),
            jax.ShapeDtypeStruct((SLABS, t, LANES), F32),
        ],
        scratch_shapes=[
            pltpu.VMEM((tm, D_MODEL), F32),
            pltpu.VMEM((tm, D_MODEL), BF16),
            pltpu.SemaphoreType.DMA(()),
        ],
        compiler_params=_cparams(("arbitrary", "arbitrary")),
        name="inproj",
    )(x2d, norm_g, w_in)


def _s5_kernel(xf_ref, xb_ref, wsf_ref, wsb_ref, wi_ref, wof_ref, wob_ref, af_ref, ab_ref,
               yf_ref, yb_ref, xs_ref, sf_ref, sb_ref, ys_ref, carry_ref, *, nb, rb):
    i = pl.program_id(2)
    half = SLAB_STATE

    @pl.when(i == 0)
    def _():
        carry_ref[...] = jnp.zeros_like(carry_ref)

    def gather_rows(x_ref):
        for j in range(S5_CHUNK):
            for b in range(nb):
                xs_ref[j, pl.ds(b, rb, stride=nb), :] = x_ref[b, pl.ds(j, rb, stride=S5_CHUNK), :]
        return jnp.concatenate([xs_ref[j] for j in range(S5_CHUNK)], axis=-1).astype(BF16)

    def scatter_rows(y, y_ref):
        for j in range(S5_CHUNK):
            ys_ref[j] = y[:, j * LANES:(j + 1) * LANES]
        for j in range(S5_CHUNK):
            for b in range(nb):
                y_ref[b, pl.ds(j, rb, stride=S5_CHUNK), :] = ys_ref[j, pl.ds(b, rb, stride=nb), :]

    def scan():
        per = max(1, SUBLANES // nb)
        rows = per * nb
        steps = rb // per
        dirs = ((sf_ref, af_ref, False), (sb_ref, ab_ref, True))
        poles = [(jnp.broadcast_to(a_ref[:, :half], (nb, half)), jnp.broadcast_to(a_ref[:, half:], (nb, half)))
                 for _, a_ref, _ in dirs]

        def body(step, carry):
            out = []
            for (s_ref, _, reverse), (a_re, a_im), (e_re, e_im) in zip(dirs, poles, carry):
                row = ((steps - 1 - step) if reverse else step) * rows
                s = s_ref[row:row + rows, :]
                entering = [None] * per
                for q in (range(per - 1, -1, -1) if reverse else range(per)):
                    sq = s[q * nb:(q + 1) * nb]
                    entering[q] = jnp.concatenate([e_re, e_im], axis=-1)
                    e_re, e_im = (a_re * e_re - a_im * e_im + sq[:, :half],
                                  a_re * e_im + a_im * e_re + sq[:, half:])
                s_ref[row:row + rows, :] = jnp.concatenate(entering, axis=0)
                out.append((e_re, e_im))
            return tuple(out)

        carry = tuple((carry_ref[slot][:, :half], carry_ref[slot][:, half:]) for slot in range(2))
        for step in range(steps):
            carry = body(step, carry)
        for slot, (e_re, e_im) in enumerate(carry):
            carry_ref[slot] = jnp.concatenate([e_re, e_im], axis=-1)

    x_f = gather_rows(xf_ref)
    sf_ref[...] = _dot(x_f, wsf_ref[...])
    x_b = gather_rows(xb_ref)
    sb_ref[...] = _dot(x_b, wsb_ref[...])
    y_intra = _dot(x_f, wi_ref[...])
    scan()
    y_f = y_intra + _dot(sf_ref[...].astype(BF16), wof_ref[...])
    scatter_rows(y_f, yf_ref)
    y_b = _dot(sb_ref[...].astype(BF16), wob_ref[...])
    scatter_rows(y_b, yb_ref)


def _s5(sx, s5w, bt, l, nb):
    lb = min(S5_ROWS * S5_CHUNK // nb, l)
    rb = lb // S5_CHUNK
    nl = l // lb
    sx4 = sx.reshape(SLABS, bt, l, LANES)
    w_in, w_intra, w_out, a = s5w
    wspec = lambda k: pl.BlockSpec((None, None, S5_K, S5_K), lambda s, g, i: (k, s, 0, 0))
    aspec = lambda k: pl.BlockSpec((None, None, 1, 2 * SLAB_STATE), lambda s, g, i: (k, s, 0, 0))
    fwd = pl.BlockSpec((None, nb, lb, LANES), lambda s, g, i: (s, g, i, 0))
    bwd = pl.BlockSpec((None, nb, lb, LANES), lambda s, g, i: (s, g, nl - 1 - i, 0))
    m = rb * nb
    yf, yb = pl.pallas_call(
        functools.partial(_s5_kernel, nb=nb, rb=rb),
        grid=(SLABS, bt // nb, nl),
        in_specs=[fwd, bwd, wspec(0), wspec(1), pl.BlockSpec((None, S5_K, S5_K), lambda s, g, i: (s, 0, 0)),
                  wspec(0), wspec(1), aspec(0), aspec(1)],
        out_specs=[fwd, bwd],
        out_shape=[jax.ShapeDtypeStruct(sx4.shape, F32)] * 2,
        scratch_shapes=[
            pltpu.VMEM((S5_CHUNK, m, LANES), F32),
            pltpu.VMEM((m, 2 * SLAB_STATE), F32),
            pltpu.VMEM((m, 2 * SLAB_STATE), F32),
            pltpu.VMEM((S5_CHUNK, m, LANES), F32),
            pltpu.VMEM((2, nb, 2 * SLAB_STATE), F32),
        ],
        compiler_params=_cparams(("parallel", "parallel", "arbitrary")),
        name="s5",
    )(sx4, sx4, w_in, w_in, w_intra, w_out, w_out, a, a)
    return yf.reshape(sx.shape), yb.reshape(sx.shape)


def _s5_weights(lam_re, lam_im, log_dt, b_re, b_im, c_re, c_im, d):
    hi = lax.Precision.HIGHEST
    tc, ng, ns, nc = S5_CHUNK, SLAB_GROUPS, SSM_STATE, SSM_GROUP
    direction = np.arange(2)[:, None]
    steps = np.arange(tc)
    dt = jnp.exp(log_dt)[..., None]
    mag = jnp.exp(lam_re * dt)
    a_re, a_im = mag * jnp.cos(lam_im * dt), mag * jnp.sin(lam_im * dt)
    den = lam_re * lam_re + lam_im * lam_im
    nr = a_re - 1.0
    k_re = (nr * lam_re + a_im * lam_im) / den
    k_im = (a_im * lam_re - nr * lam_im) / den
    tau = jnp.arange(tc + 1, dtype=F32)[None, :, None, None]
    pmag = jnp.exp((lam_re * dt)[:, None] * tau)
    p_re = pmag * jnp.cos((lam_im * dt)[:, None] * tau)
    p_im = pmag * jnp.sin((lam_im * dt)[:, None] * tau)

    group_of = lambda n, width: (jnp.arange(n, dtype=jnp.int32) // width) % ng
    own = lambda lane_group: jnp.arange(ng, dtype=jnp.int32)[:, None] == lane_group[None, :]

    sel = np.stack([tc - 1 - steps, steps])
    pq_re, pq_im = p_re[direction, sel], p_im[direction, sel]
    q_re = pq_re * k_re[:, None] - pq_im * k_im[:, None]
    q_im = pq_re * k_im[:, None] + pq_im * k_re[:, None]
    to_jlanes = lambda q: jnp.transpose(q.reshape(2, tc, SLABS, ng * ns), (0, 2, 1, 3))[:, :, :, None, None, :]
    to_clanes = lambda b: jnp.transpose(b.reshape(2, SLABS, ng, ns, nc), (0, 1, 4, 2, 3)).reshape(
        2, SLABS, 1, 1, nc, ng * ns)
    qr, qi = to_jlanes(q_re), to_jlanes(q_im)
    br, bi = to_clanes(b_re), to_clanes(b_im)
    keep = own(group_of(ng * ns, ns))[None, None, None, :, None, :]
    w_in = jnp.concatenate([jnp.where(keep, qr * br - qi * bi, 0.0), jnp.where(keep, qr * bi + qi * br, 0.0)],
                           axis=-1).astype(BF16).reshape(2, SLABS, S5_K, S5_K)

    sel = np.stack([steps + 1, tc - steps])
    lane = jnp.arange(S5_K, dtype=jnp.int32)
    spread = (jnp.arange(tc * ng, dtype=jnp.int32)[:, None]
              == ((lane // LANES) * ng + (lane % LANES) // nc)[None, :]).astype(F32)
    to_nrows = lambda p: jnp.dot(
        jnp.transpose(p[direction, sel].reshape(2, tc, SLABS, ng, ns), (0, 2, 4, 1, 3)).reshape(-1, tc * ng),
        spread, precision=hi).reshape(2, SLABS, ns, S5_K)
    to_dlanes = lambda c: jnp.tile(
        jnp.transpose(c.reshape(2, SLABS, ng, nc, ns), (0, 1, 4, 2, 3)).reshape(2, SLABS, ns, LANES), (1, 1, 1, tc))
    pr, pi = to_nrows(p_re), to_nrows(p_im)
    cr, ci = to_dlanes(c_re), to_dlanes(c_im)
    w = jnp.stack([cr * pr - ci * pi, -(cr * pi + ci * pr)], axis=2)
    keep = own(group_of(S5_K, nc))[None, None, None, :, None, :]
    w_out = jnp.where(keep, w[:, :, :, None], 0.0).astype(BF16).reshape(2, SLABS, S5_K, S5_K)

    cp_re = c_re[:, None] * p_re[:, :tc, :, None, :] - c_im[:, None] * p_im[:, :tc, :, None, :]
    cp_im = c_re[:, None] * p_im[:, :tc, :, None, :] + c_im[:, None] * p_re[:, :tc, :, None, :]
    bb_re = k_re[..., None] * b_re - k_im[..., None] * b_im
    bb_im = k_re[..., None] * b_im + k_im[..., None] * b_re
    kern = jnp.einsum('klgdn,kgnc->kgcld', jnp.concatenate([cp_re, -cp_im], axis=-1),
                      jnp.concatenate([bb_re, bb_im], axis=-2), precision=lax.Precision.HIGH)
    kern = jnp.transpose(kern.reshape(2, SLABS, ng, nc, tc, nc), (0, 1, 4, 3, 2, 5)).reshape(
        2, SLABS, tc, nc, LANES)
    chan = jnp.arange(nc, dtype=jnp.int32)
    skip = jnp.where(chan[:, None] == (jnp.arange(LANES, dtype=jnp.int32) % nc)[None, :],
                     d.reshape(SLABS, 1, LANES), 0.0)
    centre = (kern[0, :, 0] + kern[1, :, 0] + skip)[:, None]
    lags = jnp.concatenate([kern[1, :, :0:-1], centre, kern[0, :, 1:]], axis=1)
    rows = jnp.stack([jnp.concatenate([lags[:, tc - 1 + jp - j] for jp in range(tc)], axis=-1)
                      for j in range(tc)], axis=1)
    keep = own(group_of(S5_K, nc))[None, None, :, None, :]
    w_intra = jnp.where(keep, rows[:, :, None], 0.0).astype(BF16).reshape(SLABS, S5_K, S5_K)

    a = jnp.stack([p_re[:, tc], p_im[:, tc]], axis=1).reshape(2, 2, SLABS, SLAB_STATE)
    a = jnp.transpose(a, (0, 2, 1, 3)).reshape(2, SLABS, 1, 2 * SLAB_STATE)
    return w_in, w_intra, w_out, a


def _rel_bucket(rel):
    half = NUM_BUCKETS // 2
    max_exact = half // 2
    ret = (rel > 0).astype(jnp.int32) * half
    n = jnp.abs(rel)
    nf = jnp.maximum(n, 1).astype(jnp.float32)
    large = max_exact + (jnp.log(nf / max_exact) / math.log(MAX_DISTANCE / max_exact)
                         * (half - max_exact)).astype(jnp.int32)
    large = jnp.minimum(large, half - 1)
    return ret + jnp.where(n < max_exact, n, large)


def _bias_tiles(rel_bias, tile):
    assert tile >= MAX_DISTANCE
    qi = jnp.arange(tile, dtype=jnp.int32)[:, None]
    kj = jnp.arange(tile, dtype=jnp.int32)[None, :]
    offs = (jnp.arange(5, dtype=jnp.int32) - 2) * tile
    rel = offs[:, None, None] + kj[None] - qi[None]
    onehot = (_rel_bucket(rel)[None] == jnp.arange(NUM_BUCKETS, dtype=jnp.int32)[:, None, None, None])
    tab = jnp.einsum('nh,nabc->habc', rel_bias.astype(F32) * LOG2E, onehot.astype(F32),
                     precision=lax.Precision.HIGHEST)
    return tab


def _attn_kernel(lam_ref, q_ref, k_ref, v_ref, bias_ref, g_ref, o_ref,
                 sa_ref, sb_ref, ma_ref, mb_ref, *, tile, nk):
    hd = ATT_HEAD_DIM
    nt = (((1,), (1,)), ((), ()))
    lane_blocks = tile // LANES
    reps = ATT_V_DIM // LANES
    nq = nk
    assert nq % 2 == 0

    def tile_rows(i):
        return pl.ds(pl.multiple_of(i * tile, tile), tile)

    def step(i_qk, i_pv, w_bufs, r_bufs):
        if i_qk is not None:
            s_w, m_w = w_bufs
            q_next = [q_ref[tile_rows(i_qk), c * hd:(c + 1) * hd] for c in range(2)]
            mx = [jnp.full((tile, LANES), -jnp.inf, F32) for _ in range(2)]
        if i_pv is not None:
            s_r, m_r = r_bufs
            m_cur = [m_r[c] for c in range(2)]
            ls = [jnp.zeros((tile, LANES), F32) for _ in range(2)]
            acc = [None, None]
        for kj in range(nk):
            cols = slice(kj * tile, (kj + 1) * tile)
            if i_qk is not None:
                bias = bias_ref[jnp.clip(kj - i_qk, -2, 2) + 2]
                for c in range(2):
                    s = lax.dot_general(q_next[c], k_ref[cols, c * hd:(c + 1) * hd], nt,
                                        preferred_element_type=F32) + bias
                    s_w[c, :, cols] = s
                    mx[c] = jnp.maximum(mx[c], functools.reduce(
                        jnp.maximum, [s[:, b * LANES:(b + 1) * LANES] for b in range(lane_blocks)]))
            if i_pv is not None:
                vt = v_ref[cols, :]
                for c in range(2):
                    ps = []
                    for b in range(lane_blocks):
                        lo = kj * tile + b * LANES
                        ps.append(jnp.exp2(s_r[c, :, lo:lo + LANES] - m_cur[c]))
                    ls[c] = ls[c] + functools.reduce(jnp.add, ps)
                    d = _dot(jnp.concatenate([p.astype(BF16) for p in ps], axis=1), vt)
                    acc[c] = d if acc[c] is None else acc[c] + d
        if i_qk is not None:
            for c in range(2):
                m_w[c] = jnp.broadcast_to(jnp.max(mx[c], axis=1, keepdims=True), (tile, LANES))
        if i_pv is not None:
            inv0 = jnp.broadcast_to(1.0 / jnp.sum(ls[0], axis=1, keepdims=True), (tile, LANES))
            inv1 = jnp.broadcast_to(lam_ref[0] / jnp.sum(ls[1], axis=1, keepdims=True), (tile, LANES))
            o = (acc[0] * jnp.concatenate([inv0] * reps, axis=1)
                 - acc[1] * jnp.concatenate([inv1] * reps, axis=1))
            ms = jnp.mean(o * o, axis=-1, keepdims=True)
            o_ref[tile_rows(i_pv), :] = (o * lax.rsqrt(ms + EPS) * g_ref[...] * (1.0 - LAM_INIT)).astype(BF16)

    buf_a = (sa_ref, ma_ref)
    buf_b = (sb_ref, mb_ref)
    step(0, None, buf_a, None)

    def pair(j, carry):
        step(2 * j + 1, 2 * j, buf_b, buf_a)
        step(2 * j + 2, 2 * j + 1, buf_a, buf_b)
        return carry

    lax.fori_loop(0, nq // 2 - 1, pair, 0)
    step(nq - 1, nq - 2, buf_b, buf_a)
    step(None, nq - 1, None, buf_b)


def _attention(proj3, bias_tiles, subln_g, lam, tile):
    bt, l, _ = proj3.shape
    nk = l // tile
    blk = ATT_V_DIM
    seq = lambda c0: pl.BlockSpec((None, l, blk), lambda h, b: (b, 0, c0 // blk + h))
    return pl.pallas_call(
        functools.partial(_attn_kernel, tile=tile, nk=nk),
        grid=(ATT_HEADS, bt),
        in_specs=[
            pl.BlockSpec(memory_space=pltpu.SMEM),
            seq(COL_Q), seq(COL_K), seq(COL_V),
            pl.BlockSpec((None, 5, tile, tile), lambda h, b: (h, 0, 0, 0), pipeline_mode=pl.Buffered(1)),
            pl.BlockSpec((1, blk), lambda h, b: (0, 0)),
        ],
        out_specs=seq(0),
        out_shape=jax.ShapeDtypeStruct((bt, l, ATT_WIDTH), BF16),
        scratch_shapes=[
            pltpu.VMEM((2, tile, l), F32),
            pltpu.VMEM((2, tile, l), F32),
            pltpu.VMEM((2, tile, LANES), F32),
            pltpu.VMEM((2, tile, LANES), F32),
        ],
        compiler_params=_cparams(("parallel", "parallel")),
        name="attention",
    )(lam, proj3, proj3, proj3, bias_tiles, subln_g)


def _branches_kernel(yf_ref, yb_ref, z_ref, og_ref, gates_ref, gw_ref, gb_ref, ws_ref, wa_ref, o_ref):
    y = jnp.concatenate([yf_ref[s] + yb_ref[s] for s in range(SLABS)], axis=-1)
    y = jax.nn.gelu(y)
    y = y * _sigmoid(_dot(y.astype(BF16), gw_ref[...]) + gb_ref[...])
    sz = z_ref[:, :SSM_WIDTH].astype(F32)
    y = (y * (sz * _sigmoid(sz))).astype(BF16)
    y_s = _dot(y, ws_ref[...])
    az = z_ref[:, SSM_WIDTH:].astype(F32)
    o = (og_ref[...].astype(F32) * (az * _sigmoid(az))).astype(BF16)
    y_a = _dot(o, wa_ref[...])
    g_s = gates_ref[:, :D_MODEL].astype(F32)
    g_a = gates_ref[:, D_MODEL:].astype(F32)
    o_ref[...] = (_sigmoid(g_s) * y_s + _sigmoid(g_a) * y_a).astype(BF16)


def _branches(yf, yb, proj, og, glu_w, glu_b, w_s, w_a, tm):
    t = proj.shape[0]
    slab = pl.BlockSpec((SLABS, tm, LANES), lambda i: (0, i, 0))
    return pl.pallas_call(
        _branches_kernel,
        grid=(t // tm,),
        in_specs=[
            slab, slab,
            pl.BlockSpec((tm, SSM_WIDTH + ATT_WIDTH), lambda i: (i, COL_SZ // (SSM_WIDTH + ATT_WIDTH))),
            pl.BlockSpec((tm, ATT_WIDTH), lambda i: (i, 0)),
            pl.BlockSpec((tm, 2 * D_MODEL), lambda i: (i, COL_GATES // (2 * D_MODEL))),
            _const_spec((SSM_WIDTH, SSM_WIDTH)),
            _const_spec((1, SSM_WIDTH)),
            _const_spec((SSM_WIDTH, D_MODEL)),
            _const_spec((ATT_WIDTH, D_MODEL)),
        ],
        out_specs=pl.BlockSpec((tm, D_MODEL), lambda i: (i, 0)),
        out_shape=jax.ShapeDtypeStruct((t, D_MODEL), BF16),
        compiler_params=_cparams(("parallel",)),
        name="branches",
    )(yf, yb, proj, og, proj, glu_w, glu_b, w_s, w_a)


def _rms(x, g):
    ms = jnp.mean(x * x, axis=-1, keepdims=True)
    return x * lax.rsqrt(ms + EPS) * g


def _tail_kernel(m_ref, x_ref, p_ref, wo_ref, png_ref, pgw_ref, ppw_ref, fg_ref, o_ref):
    h = x_ref[...] + _dot(m_ref[...], wo_ref[...])
    gate = _sigmoid(_dot(_rms(h, png_ref[...]).astype(BF16), pgw_ref[...]))
    h = h + gate * _dot(p_ref[...].astype(BF16), ppw_ref[...])
    o_ref[...] = _rms(h, fg_ref[...])


def _tail(merged, x2d, p2d, w_out, ple_norm_g, ple_gate_w, ple_proj_w, final_g, tm):
    t = x2d.shape[0]
    row = lambda w: pl.BlockSpec((tm, w), lambda i: (i, 0))
    return pl.pallas_call(
        _tail_kernel,
        grid=(t // tm,),
        in_specs=[
            row(D_MODEL), row(D_MODEL), row(PLE_DIM),
            _const_spec((D_MODEL, D_MODEL)),
            _const_spec((1, D_MODEL)),
            _const_spec((D_MODEL, D_MODEL)),
            _const_spec((PLE_DIM, D_MODEL)),
            _const_spec((1, D_MODEL)),
        ],
        out_specs=row(D_MODEL),
        out_shape=jax.ShapeDtypeStruct((t, D_MODEL), F32),
        compiler_params=_cparams(("parallel",)),
        name="tail",
    )(merged, x2d, p2d, w_out, ple_norm_g, ple_gate_w, ple_proj_w, final_g)


def _prep_w_in(w):
    c = np.cumsum([0, SSM_WIDTH, SSM_WIDTH, ATT_QK_WIDTH, ATT_QK_WIDTH, ATT_WIDTH, ATT_WIDTH, D_MODEL, D_MODEL])
    part = lambda k: w[:, c[k]:c[k + 1]]
    return jnp.concatenate([part(6), part(7), part(1), part(5), part(2), part(3), part(4), part(0)],
                           axis=1).astype(BF16)


def _tiles(bt, l):
    t = bt * l
    tm_in = 1024 if t % 1024 == 0 else l
    tm = 512 if t % 512 == 0 else l
    nb = 8 if bt % 8 == 0 else (4 if bt % 4 == 0 else (2 if bt % 2 == 0 else 1))
    att = 256 if l % 256 == 0 else MAX_DISTANCE
    return tm_in, tm, nb, att


def _trunk(x, p, w):
    bt, l, _ = x.shape
    t = bt * l
    tm_in, tm, nb, att_tile = _tiles(bt, l)
    x2d = x.reshape(t, D_MODEL)
    proj, sx = _inproj(x2d, w['norm_g'], w['w_in'], tm_in)
    yf, yb = _s5(sx, w['s5'], bt, l, nb)
    og = _attention(proj.reshape(bt, l, PROJ_W), _bias_tiles(w['rel_bias'], att_tile), w['subln_g'],
                    w['lam'], att_tile)
    merged = _branches(yf, yb, proj, og.reshape(t, ATT_WIDTH), w['glu_w'], w['glu_b'],
                       w['w_branch_s'], w['w_branch_a'], tm)
    out = _tail(merged, x2d, p.reshape(t, PLE_DIM), w['w_out'], w['ple_norm_g'], w['ple_gate_w'],
                w['ple_proj_w'], w['final_g'], tm)
    return out.reshape(bt, l, D_MODEL)


def _prep_weights(rel_bias, norm_g, w_in, ssm_lambda_re, ssm_lambda_im, ssm_log_dt, ssm_b_re, ssm_b_im,
                  ssm_c_re, ssm_c_im, ssm_d, glu_w, glu_b, lam_q1, lam_k1, lam_q2, lam_k2, subln_g,
                  w_branch_s, w_branch_a, w_out, ple_norm_g, ple_gate_w, ple_proj_w, final_g):
    lam = (jnp.exp(jnp.sum(lam_q1[0] * lam_k1[0])) - jnp.exp(jnp.sum(lam_q2[0] * lam_k2[0])) + LAM_INIT)
    return {
        'rel_bias': rel_bias,
        'norm_g': norm_g[0].reshape(1, D_MODEL),
        'w_in': _prep_w_in(w_in[0]),
        's5': _s5_weights(ssm_lambda_re[0], ssm_lambda_im[0], ssm_log_dt[0], ssm_b_re[0], ssm_b_im[0],
                          ssm_c_re[0], ssm_c_im[0], ssm_d[0]),
        'glu_w': glu_w[0].astype(BF16),
        'glu_b': glu_b[0].reshape(1, SSM_WIDTH),
        'lam': lam.reshape(1).astype(F32),
        'subln_g': subln_g[0].reshape(1, ATT_V_DIM),
        'w_branch_s': w_branch_s[0].astype(BF16),
        'w_branch_a': w_branch_a[0].astype(BF16),
        'w_out': w_out[0].astype(BF16),
        'ple_norm_g': ple_norm_g[0].reshape(1, D_MODEL),
        'ple_gate_w': ple_gate_w[0].astype(BF16),
        'ple_proj_w': ple_proj_w[0].astype(BF16),
        'final_g': final_g.reshape(1, D_MODEL),
    }


def kernel(x_prompt, x_sample, p_prompt, p_sample, rel_bias, norm_g, w_in, ssm_lambda_re, ssm_lambda_im,
           ssm_log_dt, ssm_b_re, ssm_b_im, ssm_c_re, ssm_c_im, ssm_d, glu_w, glu_b, lam_q1, lam_k1, lam_q2,
           lam_k2, subln_g, w_branch_s, w_branch_a, w_out, ple_norm_g, ple_gate_w, ple_proj_w, final_g):
    w = _prep_weights(rel_bias, norm_g, w_in, ssm_lambda_re, ssm_lambda_im, ssm_log_dt, ssm_b_re, ssm_b_im,
                      ssm_c_re, ssm_c_im, ssm_d, glu_w, glu_b, lam_q1, lam_k1, lam_q2, lam_k2, subln_g,
                      w_branch_s, w_branch_a, w_out, ple_norm_g, ple_gate_w, ple_proj_w, final_g)
    return (_trunk(x_prompt, p_prompt[0], w), _trunk(x_sample, p_sample[0], w))
```

```python
import functools
import math

import jax
import jax.numpy as jnp
import numpy as np
from jax import lax
from jax.experimental import pallas as pl
from jax.experimental.pallas import tpu as pltpu

F32 = jnp.float32
BF16 = jnp.bfloat16

D_MODEL = 2048
SSM_WIDTH = 1024
SSM_GROUP = 16
SSM_GROUPS = SSM_WIDTH // SSM_GROUP
SSM_STATE = 64
ATT_HEADS = 4
ATT_HEAD_DIM = 128
ATT_V_DIM = 2 * ATT_HEAD_DIM
ATT_QK_WIDTH = ATT_HEADS * 2 * ATT_HEAD_DIM
ATT_WIDTH = ATT_HEADS * ATT_V_DIM
NUM_BUCKETS = 32
MAX_DISTANCE = 128
PLE_DIM = 256
EPS = 1e-6
LAM_INIT = 0.8 - 0.6 * math.exp(-0.3 * 0)
LOG2E = math.log2(math.e)
ATT_SCALE = ATT_HEAD_DIM ** -0.5 * LOG2E

LANES = 128
SUBLANES = 8
SLABS = SSM_WIDTH // LANES
SLAB_GROUPS = LANES // SSM_GROUP
S5_CHUNK = 8
SLAB_STATE = SLAB_GROUPS * SSM_STATE
S5_K = S5_CHUNK * LANES
S5_ROWS = 512

IN_WIDTH = 2 * SSM_WIDTH + 2 * ATT_QK_WIDTH + 2 * ATT_WIDTH + 2 * D_MODEL
PROJ_W = IN_WIDTH
COL_SX = 0
COL_SZ = COL_SX + SSM_WIDTH
COL_Q = COL_SZ + SSM_WIDTH
COL_K = COL_Q + ATT_QK_WIDTH
COL_V = COL_K + ATT_QK_WIDTH
COL_AZ = COL_V + ATT_WIDTH
COL_GS = COL_AZ + ATT_WIDTH
COL_GA = COL_GS + D_MODEL
IN_TN = 2048
VMEM_LIMIT = 56 * 1024 * 1024


def _cparams(sem):
    return pltpu.CompilerParams(dimension_semantics=sem, vmem_limit_bytes=VMEM_LIMIT)


def _dot(a, b):
    return jnp.dot(a, b, preferred_element_type=F32)


def _sigmoid(x):
    return 0.5 * jnp.tanh(0.5 * x) + 0.5


def _const_spec(shape):
    zeros = (0,) * len(shape)
    return pl.BlockSpec(shape, lambda *_: zeros, pipeline_mode=pl.Buffered(1))


def _inproj_kernel(x_hbm, g_ref, w_ref, proj_ref, sx_ref, xbuf_ref, hn_ref, sem, *, tm):
    i = pl.program_id(0)
    j = pl.program_id(1)

    def x_copy(tile):
        return pltpu.make_async_copy(x_hbm.at[pl.ds(tile * tm, tm)], xbuf_ref, sem)

    @pl.when(j == 0)
    def _():
        @pl.when(i == 0)
        def _():
            x_copy(0).start()

        x_copy(i).wait()
        x = xbuf_ref[...]
        ms = jnp.mean(x * x, axis=-1, keepdims=True)
        hn_ref[...] = (x * lax.rsqrt(ms + EPS) * g_ref[...]).astype(BF16)

        @pl.when(i + 1 < pl.num_programs(0))
        def _():
            x_copy(i + 1).start()

    acc = _dot(hn_ref[...], w_ref[...])
    col = lax.broadcasted_iota(jnp.int32, (1, IN_TN), 1)
    scale = jnp.where((j == COL_Q // IN_TN) & (col < ATT_QK_WIDTH), ATT_SCALE, 1.0).astype(F32)
    proj_ref[...] = (acc * scale).astype(BF16)

    @pl.when(j == COL_SX // IN_TN)
    def _():
        base = COL_SX % IN_TN
        for s in range(SLABS):
            sx_ref[s] = acc[:, base + s * LANES:base + (s + 1) * LANES]


def _inproj(x2d, norm_g, w_in, tm):
    t = x2d.shape[0]
    return pl.pallas_call(
        functools.partial(_inproj_kernel, tm=tm),
        grid=(t // tm, IN_WIDTH // IN_TN),
        in_specs=[
            pl.BlockSpec(memory_space=pl.ANY),
            pl.BlockSpec((1, D_MODEL), lambda i, j: (0, 0)),
            pl.BlockSpec((D_MODEL, IN_TN), lambda i, j: (0, j)),
        ],
        out_specs=[
            pl.BlockSpec((tm, IN_TN), lambda i, j: (i, j)),
            pl.BlockSpec((SLABS, tm, LANES), lambda i, j: (0, i, 0)),
        ],
        out_shape=[
            jax.ShapeDtypeStruct((t, PROJ_W), BF16),
            jax.ShapeDtypeStruct((SLABS, t, LANES), F32),
        ],
        scratch_shapes=[
            pltpu.VMEM((tm, D_MODEL), F32),
            pltpu.VMEM((tm, D_MODEL), BF16),
            pltpu.SemaphoreType.DMA(()),
        ],
        compiler_params=_cparams(("arbitrary", "arbitrary")),
        name="inproj",
    )(x2d, norm_g, w_in)


def _s5_kernel(xf_ref, xb_ref, wsf_ref, wsb_ref, wi_ref, wof_ref, wob_ref, af_ref, ab_ref,
               yf_ref, yb_ref, xs_ref, sf_ref, sb_ref, ys_ref, carry_ref, *, nb, rb):
    i = pl.program_id(2)
    half = SLAB_STATE

    @pl.when(i == 0)
    def _():
        carry_ref[...] = jnp.zeros_like(carry_ref)

    def gather_rows(x_ref):
        for j in range(S5_CHUNK):
            for b in range(nb):
                xs_ref[j, pl.ds(b, rb, stride=nb), :] = x_ref[b, pl.ds(j, rb, stride=S5_CHUNK), :]
        return jnp.concatenate([xs_ref[j] for j in range(S5_CHUNK)], axis=-1).astype(BF16)

    def scatter_rows(y, y_ref):
        for j in range(S5_CHUNK):
            ys_ref[j] = y[:, j * LANES:(j + 1) * LANES]
        for j in range(S5_CHUNK):
            for b in range(nb):
                y_ref[b, pl.ds(j, rb, stride=S5_CHUNK), :] = ys_ref[j, pl.ds(b, rb, stride=nb), :]

    def scan():
        per = max(1, SUBLANES // nb)
        rows = per * nb
        steps = rb // per
        dirs = ((sf_ref, af_ref, False), (sb_ref, ab_ref, True))
        poles = [(jnp.broadcast_to(a_ref[:, :half], (nb, half)), jnp.broadcast_to(a_ref[:, half:], (nb, half)))
                 for _, a_ref, _ in dirs]

        def body(step, carry):
            out = []
            for (s_ref, _, reverse), (a_re, a_im), (e_re, e_im) in zip(dirs, poles, carry):
                row = ((steps - 1 - step) if reverse else step) * rows
                s = s_ref[row:row + rows, :]
                entering = [None] * per
                for q in (range(per - 1, -1, -1) if reverse else range(per)):
                    sq = s[q * nb:(q + 1) * nb]
                    entering[q] = jnp.concatenate([e_re, e_im], axis=-1)
                    e_re, e_im = (a_re * e_re - a_im * e_im + sq[:, :half],
                                  a_re * e_im + a_im * e_re + sq[:, half:])
                s_ref[row:row + rows, :] = jnp.concatenate(entering, axis=0)
                out.append((e_re, e_im))
            return tuple(out)

        carry = tuple((carry_ref[slot][:, :half], carry_ref[slot][:, half:]) for slot in range(2))
        for step in range(steps):
            carry = body(step, carry)
        for slot, (e_re, e_im) in enumerate(carry):
            carry_ref[slot] = jnp.concatenate([e_re, e_im], axis=-1)

    x_f = gather_rows(xf_ref)
    sf_ref[...] = _dot(x_f, wsf_ref[...])
    x_b = gather_rows(xb_ref)
    sb_ref[...] = _dot(x_b, wsb_ref[...])
    y_intra = _dot(x_f, wi_ref[...])
    scan()
    y_f = y_intra + _dot(sf_ref[...].astype(BF16), wof_ref[...])
    scatter_rows(y_f, yf_ref)
    y_b = _dot(sb_ref[...].astype(BF16), wob_ref[...])
    scatter_rows(y_b, yb_ref)


def _s5(sx, s5w, bt, l, nb):
    lb = min(S5_ROWS * S5_CHUNK // nb, l)
    rb = lb // S5_CHUNK
    nl = l // lb
    sx4 = sx.reshape(SLABS, bt, l, LANES)
    w_in, w_intra, w_out, a = s5w
    wspec = lambda k: pl.BlockSpec((None, None, S5_K, S5_K), lambda s, g, i: (k, s, 0, 0))
    aspec = lambda k: pl.BlockSpec((None, None, 1, 2 * SLAB_STATE), lambda s, g, i: (k, s, 0, 0))
    fwd = pl.BlockSpec((None, nb, lb, LANES), lambda s, g, i: (s, g, i, 0))
    bwd = pl.BlockSpec((None, nb, lb, LANES), lambda s, g, i: (s, g, nl - 1 - i, 0))
    m = rb * nb
    yf, yb = pl.pallas_call(
        functools.partial(_s5_kernel, nb=nb, rb=rb),
        grid=(SLABS, bt // nb, nl),
        in_specs=[fwd, bwd, wspec(0), wspec(1), pl.BlockSpec((None, S5_K, S5_K), lambda s, g, i: (s, 0, 0)),
                  wspec(0), wspec(1), aspec(0), aspec(1)],
        out_specs=[fwd, bwd],
        out_shape=[jax.ShapeDtypeStruct(sx4.shape, F32)] * 2,
        scratch_shapes=[
            pltpu.VMEM((S5_CHUNK, m, LANES), F32),
            pltpu.VMEM((m, 2 * SLAB_STATE), F32),
            pltpu.VMEM((m, 2 * SLAB_STATE), F32),
            pltpu.VMEM((S5_CHUNK, m, LANES), F32),
            pltpu.VMEM((2, nb, 2 * SLAB_STATE), F32),
        ],
        compiler_params=_cparams(("parallel", "parallel", "arbitrary")),
        name="s5",
    )(sx4, sx4, w_in, w_in, w_intra, w_out, w_out, a, a)
    return yf.reshape(sx.shape), yb.reshape(sx.shape)


def _s5_weights(lam_re, lam_im, log_dt, b_re, b_im, c_re, c_im, d):
    hi = lax.Precision.HIGHEST
    tc, ng, ns, nc = S5_CHUNK, SLAB_GROUPS, SSM_STATE, SSM_GROUP
    direction = np.arange(2)[:, None]
    steps = np.arange(tc)
    dt = jnp.exp(log_dt)[..., None]
    mag = jnp.exp(lam_re * dt)
    a_re, a_im = mag * jnp.cos(lam_im * dt), mag * jnp.sin(lam_im * dt)
    den = lam_re * lam_re + lam_im * lam_im
    nr = a_re - 1.0
    k_re = (nr * lam_re + a_im * lam_im) / den
    k_im = (a_im * lam_re - nr * lam_im) / den
    tau = jnp.arange(tc + 1, dtype=F32)[None, :, None, None]
    pmag = jnp.exp((lam_re * dt)[:, None] * tau)
    p_re = pmag * jnp.cos((lam_im * dt)[:, None] * tau)
    p_im = pmag * jnp.sin((lam_im * dt)[:, None] * tau)

    group_of = lambda n, width: (jnp.arange(n, dtype=jnp.int32) // width) % ng
    own = lambda lane_group: jnp.arange(ng, dtype=jnp.int32)[:, None] == lane_group[None, :]

    sel = np.stack([tc - 1 - steps, steps])
    pq_re, pq_im = p_re[direction, sel], p_im[direction, sel]
    q_re = pq_re * k_re[:, None] - pq_im * k_im[:, None]
    q_im = pq_re * k_im[:, None] + pq_im * k_re[:, None]
    to_jlanes = lambda q: jnp.transpose(q.reshape(2, tc, SLABS, ng * ns), (0, 2, 1, 3))[:, :, :, None, None, :]
    to_clanes = lambda b: jnp.transpose(b.reshape(2, SLABS, ng, ns, nc), (0, 1, 4, 2, 3)).reshape(
        2, SLABS, 1, 1, nc, ng * ns)
    qr, qi = to_jlanes(q_re), to_jlanes(q_im)
    br, bi = to_clanes(b_re), to_clanes(b_im)
    keep = own(group_of(ng * ns, ns))[None, None, None, :, None, :]
    w_in = jnp.concatenate([jnp.where(keep, qr * br - qi * bi, 0.0), jnp.where(keep, qr * bi + qi * br, 0.0)],
                           axis=-1).astype(BF16).reshape(2, SLABS, S5_K, S5_K)

    sel = np.stack([steps + 1, tc - steps])
    lane = jnp.arange(S5_K, dtype=jnp.int32)
    spread = (jnp.arange(tc * ng, dtype=jnp.int32)[:, None]
              == ((lane // LANES) * ng + (lane % LANES) // nc)[None, :]).astype(F32)
    to_nrows = lambda p: jnp.dot(
        jnp.transpose(p[direction, sel].reshape(2, tc, SLABS, ng, ns), (0, 2, 4, 1, 3)).reshape(-1, tc * ng),
        spread, precision=hi).reshape(2, SLABS, ns, S5_K)
    to_dlanes = lambda c: jnp.tile(
        jnp.transpose(c.reshape(2, SLABS, ng, nc, ns), (0, 1, 4, 2, 3)).reshape(2, SLABS, ns, LANES), (1, 1, 1, tc))
    pr, pi = to_nrows(p_re), to_nrows(p_im)
    cr, ci = to_dlanes(c_re), to_dlanes(c_im)
    w = jnp.stack([cr * pr - ci * pi, -(cr * pi + ci * pr)], axis=2)
    keep = own(group_of(S5_K, nc))[None, None, None, :, None, :]
    w_out = jnp.where(keep, w[:, :, :, None], 0.0).astype(BF16).reshape(2, SLABS, S5_K, S5_K)

    cp_re = c_re[:, None] * p_re[:, :tc, :, None, :] - c_im[:, None] * p_im[:, :tc, :, None, :]
    cp_im = c_re[:, None] * p_im[:, :tc, :, None, :] + c_im[:, None] * p_re[:, :tc, :, None, :]
    bb_re = k_re[..., None] * b_re - k_im[..., None] * b_im
    bb_im = k_re[..., None] * b_im + k_im[..., None] * b_re
    kern = jnp.einsum('klgdn,kgnc->kgcld', jnp.concatenate([cp_re, -cp_im], axis=-1),
                      jnp.concatenate([bb_re, bb_im], axis=-2), precision=lax.Precision.HIGH)
    kern = jnp.transpose(kern.reshape(2, SLABS, ng, nc, tc, nc), (0, 1, 4, 3, 2, 5)).reshape(
        2, SLABS, tc, nc, LANES)
    chan = jnp.arange(nc, dtype=jnp.int32)
    skip = jnp.where(chan[:, None] == (jnp.arange(LANES, dtype=jnp.int32) % nc)[None, :],
                     d.reshape(SLABS, 1, LANES), 0.0)
    centre = (kern[0, :, 0] + kern[1, :, 0] + skip)[:, None]
    lags = jnp.concatenate([kern[1, :, :0:-1], centre, kern[0, :, 1:]], axis=1)
    rows = jnp.stack([jnp.concatenate([lags[:, tc - 1 + jp - j] for jp in range(tc)], axis=-1)
                      for j in range(tc)], axis=1)
    keep = own(group_of(S5_K, nc))[None, None, :, None, :]
    w_intra = jnp.where(keep, rows[:, :, None], 0.0).astype(BF16).reshape(SLABS, S5_K, S5_K)

    a = jnp.stack([p_re[:, tc], p_im[:, tc]], axis=1).reshape(2, 2, SLABS, SLAB_STATE)
    a = jnp.transpose(a, (0, 2, 1, 3)).reshape(2, SLABS, 1, 2 * SLAB_STATE)
    return w_in, w_intra, w_out, a


def _rel_bucket(rel):
    half = NUM_BUCKETS // 2
    max_exact = half // 2
    ret = (rel > 0).astype(jnp.int32) * half
    n = jnp.abs(rel)
    nf = jnp.maximum(n, 1).astype(jnp.float32)
    large = max_exact + (jnp.log(nf / max_exact) / math.log(MAX_DISTANCE / max_exact)
                         * (half - max_exact)).astype(jnp.int32)
    large = jnp.minimum(large, half - 1)
    return ret + jnp.where(n < max_exact, n, large)


def _bias_tiles(rel_bias, tile):
    assert tile >= MAX_DISTANCE
    qi = jnp.arange(tile, dtype=jnp.int32)[:, None]
    kj = jnp.arange(tile, dtype=jnp.int32)[None, :]
    offs = (jnp.arange(5, dtype=jnp.int32) - 2) * tile
    rel = offs[:, None, None] + kj[None] - qi[None]
    bucket = _rel_bucket(rel)[None]
    table = rel_bias.astype(F32) * LOG2E
    tab = jnp.zeros((ATT_HEADS,) + rel.shape, F32)
    for n in range(NUM_BUCKETS):
        tab = jnp.where(bucket == n, table[n][:, None, None, None], tab)
    return tab


def _attn_kernel(lam_ref, q_ref, k_ref, v_ref, bias_ref, g_ref, o_ref,
                 sa_ref, sb_ref, ma_ref, mb_ref, *, tile, nk):
    hd = ATT_HEAD_DIM
    nt = (((1,), (1,)), ((), ()))
    lane_blocks = tile // LANES
    reps = ATT_V_DIM // LANES
    nq = nk
    assert nq % 2 == 0

    def tile_rows(i):
        return pl.ds(pl.multiple_of(i * tile, tile), tile)

    def step(i_qk, i_pv, w_bufs, r_bufs):
        if i_qk is not None:
            s_w, m_w = w_bufs
            q_next = [q_ref[tile_rows(i_qk), c * hd:(c + 1) * hd] for c in range(2)]
            mx = [jnp.full((tile, LANES), -jnp.inf, F32) for _ in range(2)]
        if i_pv is not None:
            s_r, m_r = r_bufs
            m_cur = [m_r[c] for c in range(2)]
            ls = [jnp.zeros((tile, LANES), F32) for _ in range(2)]
            acc = [None, None]
        for kj in range(nk):
            cols = slice(kj * tile, (kj + 1) * tile)
            if i_qk is not None:
                bias = bias_ref[jnp.clip(kj - i_qk, -2, 2) + 2]
                for c in range(2):
                    s = lax.dot_general(q_next[c], k_ref[cols, c * hd:(c + 1) * hd], nt,
                                        preferred_element_type=F32) + bias
                    s_w[c, :, cols] = s
                    mx[c] = jnp.maximum(mx[c], functools.reduce(
                        jnp.maximum, [s[:, b * LANES:(b + 1) * LANES] for b in range(lane_blocks)]))
            if i_pv is not None:
                vt = v_ref[cols, :]
                for c in range(2):
                    ps = []
                    for b in range(lane_blocks):
                        lo = kj * tile + b * LANES
                        ps.append(jnp.exp2(s_r[c, :, lo:lo + LANES] - m_cur[c]))
                    ls[c] = ls[c] + functools.reduce(jnp.add, ps)
                    d = _dot(jnp.concatenate([p.astype(BF16) for p in ps], axis=1), vt)
                    acc[c] = d if acc[c] is None else acc[c] + d
        if i_qk is not None:
            for c in range(2):
                m_w[c] = jnp.broadcast_to(jnp.max(mx[c], axis=1, keepdims=True), (tile, LANES))
        if i_pv is not None:
            inv0 = jnp.broadcast_to(1.0 / jnp.sum(ls[0], axis=1, keepdims=True), (tile, LANES))
            inv1 = jnp.broadcast_to(lam_ref[0] / jnp.sum(ls[1], axis=1, keepdims=True), (tile, LANES))
            o = (acc[0] * jnp.concatenate([inv0] * reps, axis=1)
                 - acc[1] * jnp.concatenate([inv1] * reps, axis=1))
            ms = jnp.mean(o * o, axis=-1, keepdims=True)
            o_ref[tile_rows(i_pv), :] = (o * lax.rsqrt(ms + EPS) * g_ref[...] * (1.0 - LAM_INIT)).astype(BF16)

    buf_a = (sa_ref, ma_ref)
    buf_b = (sb_ref, mb_ref)
    step(0, None, buf_a, None)

    def pair(j, carry):
        step(2 * j + 1, 2 * j, buf_b, buf_a)
        step(2 * j + 2, 2 * j + 1, buf_a, buf_b)
        return carry

    lax.fori_loop(0, nq // 2 - 1, pair, 0)
    step(nq - 1, nq - 2, buf_b, buf_a)
    step(None, nq - 1, None, buf_b)


def _attention(proj3, bias_tiles, subln_g, lam, tile):
    bt, l, _ = proj3.shape
    nk = l // tile
    blk = ATT_V_DIM
    seq = lambda c0: pl.BlockSpec((None, l, blk), lambda h, b: (b, 0, c0 // blk + h))
    return pl.pallas_call(
        functools.partial(_attn_kernel, tile=tile, nk=nk),
        grid=(ATT_HEADS, bt),
        in_specs=[
            pl.BlockSpec(memory_space=pltpu.SMEM),
            seq(COL_Q), seq(COL_K), seq(COL_V),
            pl.BlockSpec((None, 5, tile, tile), lambda h, b: (h, 0, 0, 0), pipeline_mode=pl.Buffered(1)),
            pl.BlockSpec((1, blk), lambda h, b: (0, 0)),
        ],
        out_specs=seq(0),
        out_shape=jax.ShapeDtypeStruct((bt, l, ATT_WIDTH), BF16),
        scratch_shapes=[
            pltpu.VMEM((2, tile, l), F32),
            pltpu.VMEM((2, tile, l), F32),
            pltpu.VMEM((2, tile, LANES), F32),
            pltpu.VMEM((2, tile, LANES), F32),
        ],
        compiler_params=_cparams(("parallel", "parallel")),
        name="attention",
    )(lam, proj3, proj3, proj3, bias_tiles, subln_g)


def _branches_kernel(yf_ref, yb_ref, sz_ref, az_ref, og_ref, gs_ref, ga_ref, gw_ref, gb_ref, ws_ref, wa_ref,
                     o_ref):
    y = jnp.concatenate([yf_ref[s] + yb_ref[s] for s in range(SLABS)], axis=-1)
    y = jax.nn.gelu(y)
    y = y * _sigmoid(_dot(y.astype(BF16), gw_ref[...]) + gb_ref[...])
    sz = sz_ref[...].astype(F32)
    y = (y * (sz * _sigmoid(sz))).astype(BF16)
    y_s = _dot(y, ws_ref[...])
    az = az_ref[...].astype(F32)
    o = (og_ref[...].astype(F32) * (az * _sigmoid(az))).astype(BF16)
    y_a = _dot(o, wa_ref[...])
    g_s = gs_ref[...].astype(F32)
    g_a = ga_ref[...].astype(F32)
    o_ref[...] = (_sigmoid(g_s) * y_s + _sigmoid(g_a) * y_a).astype(BF16)


def _branches(yf, yb, proj, og, glu_w, glu_b, w_s, w_a, tm):
    t = proj.shape[0]
    slab = pl.BlockSpec((SLABS, tm, LANES), lambda i: (0, i, 0))
    cols = lambda c0, width: pl.BlockSpec((tm, width), lambda i: (i, c0 // width))
    return pl.pallas_call(
        _branches_kernel,
        grid=(t // tm,),
        in_specs=[
            slab, slab,
            cols(COL_SZ, SSM_WIDTH), cols(COL_AZ, ATT_WIDTH),
            pl.BlockSpec((tm, ATT_WIDTH), lambda i: (i, 0)),
            cols(COL_GS, D_MODEL), cols(COL_GA, D_MODEL),
            _const_spec((SSM_WIDTH, SSM_WIDTH)),
            _const_spec((1, SSM_WIDTH)),
            _const_spec((SSM_WIDTH, D_MODEL)),
            _const_spec((ATT_WIDTH, D_MODEL)),
        ],
        out_specs=pl.BlockSpec((tm, D_MODEL), lambda i: (i, 0)),
        out_shape=jax.ShapeDtypeStruct((t, D_MODEL), BF16),
        compiler_params=_cparams(("parallel",)),
        name="branches",
    )(yf, yb, proj, proj, og, proj, proj, glu_w, glu_b, w_s, w_a)


def _rms(x, g):
    ms = jnp.mean(x * x, axis=-1, keepdims=True)
    return x * lax.rsqrt(ms + EPS) * g


def _tail_kernel(m_ref, x_ref, p_ref, wo_ref, png_ref, pgw_ref, ppw_ref, fg_ref, o_ref):
    h = x_ref[...] + _dot(m_ref[...], wo_ref[...])
    gate = _sigmoid(_dot(_rms(h, png_ref[...]).astype(BF16), pgw_ref[...]))
    h = h + gate * _dot(p_ref[...].astype(BF16), ppw_ref[...])
    o_ref[...] = _rms(h, fg_ref[...])


def _tail(merged, x2d, p2d, w_out, ple_norm_g, ple_gate_w, ple_proj_w, final_g, tm):
    t = x2d.shape[0]
    row = lambda w: pl.BlockSpec((tm, w), lambda i: (i, 0))
    return pl.pallas_call(
        _tail_kernel,
        grid=(t // tm,),
        in_specs=[
            row(D_MODEL), row(D_MODEL), row(PLE_DIM),
            _const_spec((D_MODEL, D_MODEL)),
            _const_spec((1, D_MODEL)),
            _const_spec((D_MODEL, D_MODEL)),
            _const_spec((PLE_DIM, D_MODEL)),
            _const_spec((1, D_MODEL)),
        ],
        out_specs=row(D_MODEL),
        out_shape=jax.ShapeDtypeStruct((t, D_MODEL), F32),
        compiler_params=_cparams(("parallel",)),
        name="tail",
    )(merged, x2d, p2d, w_out, ple_norm_g, ple_gate_w, ple_proj_w, final_g)


def _tiles(bt, l):
    t = bt * l
    tm_in = 1024 if t % 1024 == 0 else l
    tm = 512 if t % 512 == 0 else l
    nb = 8 if bt % 8 == 0 else (4 if bt % 4 == 0 else (2 if bt % 2 == 0 else 1))
    att = 256 if l % 256 == 0 else MAX_DISTANCE
    return tm_in, tm, nb, att


def _trunk(x, p, w):
    bt, l, _ = x.shape
    t = bt * l
    tm_in, tm, nb, att_tile = _tiles(bt, l)
    x2d = x.reshape(t, D_MODEL)
    proj, sx = _inproj(x2d, w['norm_g'], w['w_in'], tm_in)
    yf, yb = _s5(sx, w['s5'], bt, l, nb)
    og = _attention(proj.reshape(bt, l, PROJ_W), _bias_tiles(w['rel_bias'], att_tile), w['subln_g'],
                    w['lam'], att_tile)
    merged = _branches(yf, yb, proj, og.reshape(t, ATT_WIDTH), w['glu_w'], w['glu_b'],
                       w['w_branch_s'], w['w_branch_a'], tm)
    out = _tail(merged, x2d, p.reshape(t, PLE_DIM), w['w_out'], w['ple_norm_g'], w['ple_gate_w'],
                w['ple_proj_w'], w['final_g'], tm)
    return out.reshape(bt, l, D_MODEL)


def _prep_weights(rel_bias, norm_g, w_in, ssm_lambda_re, ssm_lambda_im, ssm_log_dt, ssm_b_re, ssm_b_im,
                  ssm_c_re, ssm_c_im, ssm_d, glu_w, glu_b, lam_q1, lam_k1, lam_q2, lam_k2, subln_g,
                  w_branch_s, w_branch_a, w_out, ple_norm_g, ple_gate_w, ple_proj_w, final_g):
    lam = (jnp.exp(jnp.sum(lam_q1[0] * lam_k1[0])) - jnp.exp(jnp.sum(lam_q2[0] * lam_k2[0])) + LAM_INIT)
    return {
        'rel_bias': rel_bias,
        'norm_g': norm_g[0].reshape(1, D_MODEL),
        'w_in': w_in[0].astype(BF16),
        's5': _s5_weights(ssm_lambda_re[0], ssm_lambda_im[0], ssm_log_dt[0], ssm_b_re[0], ssm_b_im[0],
                          ssm_c_re[0], ssm_c_im[0], ssm_d[0]),
        'glu_w': glu_w[0].astype(BF16),
        'glu_b': glu_b[0].reshape(1, SSM_WIDTH),
        'lam': lam.reshape(1).astype(F32),
        'subln_g': subln_g[0].reshape(1, ATT_V_DIM),
        'w_branch_s': w_branch_s[0].astype(BF16),
        'w_branch_a': w_branch_a[0].astype(BF16),
        'w_out': w_out[0].astype(BF16),
        'ple_norm_g': ple_norm_g[0].reshape(1, D_MODEL),
        'ple_gate_w': ple_gate_w[0].astype(BF16),
        'ple_proj_w': ple_proj_w[0].astype(BF16),
        'final_g': final_g.reshape(1, D_MODEL),
    }


def kernel(x_prompt, x_sample, p_prompt, p_sample, rel_bias, norm_g, w_in, ssm_lambda_re, ssm_lambda_im,
           ssm_log_dt, ssm_b_re, ssm_b_im, ssm_c_re, ssm_c_im, ssm_d, glu_w, glu_b, lam_q1, lam_k1, lam_q2,
           lam_k2, subln_g, w_branch_s, w_branch_a, w_out, ple_norm_g, ple_gate_w, ple_proj_w, final_g):
    w = _prep_weights(rel_bias, norm_g, w_in, ssm_lambda_re, ssm_lambda_im, ssm_log_dt, ssm_b_re, ssm_b_im,
                      ssm_c_re, ssm_c_im, ssm_d, glu_w, glu_b, lam_q1, lam_k1, lam_q2, lam_k2, subln_g,
                      w_branch_s, w_branch_a, w_out, ple_norm_g, ple_gate_w, ple_proj_w, final_g)
    return (_trunk(x_prompt, p_prompt[0], w), _trunk(x_sample, p_sample[0], w))
```

```python
import functools
import math

import jax
import jax.numpy as jnp
import numpy as np
from jax import lax
from jax.experimental import pallas as pl
from jax.experimental.pallas import tpu as pltpu

F32 = jnp.float32
BF16 = jnp.bfloat16

D_MODEL = 2048
SSM_WIDTH = 1024
SSM_GROUP = 16
SSM_GROUPS = SSM_WIDTH // SSM_GROUP
SSM_STATE = 64
ATT_HEADS = 4
ATT_HEAD_DIM = 128
ATT_V_DIM = 2 * ATT_HEAD_DIM
ATT_QK_WIDTH = ATT_HEADS * 2 * ATT_HEAD_DIM
ATT_WIDTH = ATT_HEADS * ATT_V_DIM
NUM_BUCKETS = 32
MAX_DISTANCE = 128
PLE_DIM = 256
EPS = 1e-6
LAM_INIT = 0.8 - 0.6 * math.exp(-0.3 * 0)
LOG2E = math.log2(math.e)
ATT_SCALE = ATT_HEAD_DIM ** -0.5 * LOG2E

LANES = 128
SUBLANES = 8
SLABS = SSM_WIDTH // LANES
SLAB_GROUPS = LANES // SSM_GROUP
S5_CHUNK = 8
SLAB_STATE = SLAB_GROUPS * SSM_STATE
S5_K = S5_CHUNK * LANES
S5_ROWS = 512

IN_WIDTH = 2 * SSM_WIDTH + 2 * ATT_QK_WIDTH + 2 * ATT_WIDTH + 2 * D_MODEL
PROJ_W = IN_WIDTH
COL_SX = 0
COL_SZ = COL_SX + SSM_WIDTH
COL_Q = COL_SZ + SSM_WIDTH
COL_K = COL_Q + ATT_QK_WIDTH
COL_V = COL_K + ATT_QK_WIDTH
COL_AZ = COL_V + ATT_WIDTH
COL_GS = COL_AZ + ATT_WIDTH
COL_GA = COL_GS + D_MODEL
IN_TN = 2048
VMEM_LIMIT = 56 * 1024 * 1024


def _cparams(sem):
    return pltpu.CompilerParams(dimension_semantics=sem, vmem_limit_bytes=VMEM_LIMIT)


def _dot(a, b):
    return jnp.dot(a, b, preferred_element_type=F32)


def _sigmoid(x):
    return 0.5 * jnp.tanh(0.5 * x) + 0.5


def _const_spec(shape):
    zeros = (0,) * len(shape)
    return pl.BlockSpec(shape, lambda *_: zeros, pipeline_mode=pl.Buffered(1))


def _inproj_kernel(x_hbm, g_ref, w_ref, proj_ref, sx_ref, xbuf_ref, hn_ref, sem, *, tm):
    i = pl.program_id(0)
    j = pl.program_id(1)

    def x_copy(tile):
        return pltpu.make_async_copy(x_hbm.at[pl.ds(tile * tm, tm)], xbuf_ref, sem)

    @pl.when(j == 0)
    def _():
        @pl.when(i == 0)
        def _():
            x_copy(0).start()

        x_copy(i).wait()
        x = xbuf_ref[...]
        ms = jnp.mean(x * x, axis=-1, keepdims=True)
        hn_ref[...] = (x * lax.rsqrt(ms + EPS) * g_ref[...]).astype(BF16)

        @pl.when(i + 1 < pl.num_programs(0))
        def _():
            x_copy(i + 1).start()

    acc = _dot(hn_ref[...], w_ref[...])
    col = lax.broadcasted_iota(jnp.int32, (1, IN_TN), 1)
    scale = jnp.where((j == COL_Q // IN_TN) & (col < ATT_QK_WIDTH), ATT_SCALE, 1.0).astype(F32)
    proj_ref[...] = (acc * scale).astype(BF16)

    @pl.when(j == COL_SX // IN_TN)
    def _():
        base = COL_SX % IN_TN
        for s in range(SLABS):
            sx_ref[s] = acc[:, base + s * LANES:base + (s + 1) * LANES]


def _inproj(x2d, norm_g, w_in, tm):
    t = x2d.shape[0]
    return pl.pallas_call(
        functools.partial(_inproj_kernel, tm=tm),
        grid=(t // tm, IN_WIDTH // IN_TN),
        in_specs=[
            pl.BlockSpec(memory_space=pl.ANY),
            pl.BlockSpec((1, D_MODEL), lambda i, j: (0, 0)),
            pl.BlockSpec((D_MODEL, IN_TN), lambda i, j: (0, j)),
        ],
        out_specs=[
            pl.BlockSpec((tm, IN_TN), lambda i, j: (i, j)),
            pl.BlockSpec((SLABS, tm, LANES), lambda i, j: (0, i, 0)),
        ],
        out_shape=[
            jax.ShapeDtypeStruct((t, PROJ_W), BF16),
            jax.ShapeDtypeStruct((SLABS, t, LANES), F32),
        ],
        scratch_shapes=[
            pltpu.VMEM((tm, D_MODEL), F32),
            pltpu.VMEM((tm, D_MODEL), BF16),
            pltpu.SemaphoreType.DMA(()),
        ],
        compiler_params=_cparams(("arbitrary", "arbitrary")),
        name="inproj",
    )(x2d, norm_g, w_in)


def _s5_kernel(xf_ref, xb_ref, wsf_ref, wsb_ref, wi_ref, wof_ref, wob_ref, af_ref, ab_ref,
               yf_ref, yb_ref, xs_ref, sf_ref, sb_ref, ys_ref, carry_ref, *, nb, rb):
    i = pl.program_id(2)
    half = SLAB_STATE

    @pl.when(i == 0)
    def _():
        carry_ref[...] = jnp.zeros_like(carry_ref)

    def gather_rows(x_ref):
        for j in range(S5_CHUNK):
            for b in range(nb):
                xs_ref[j, pl.ds(b, rb, stride=nb), :] = x_ref[b, pl.ds(j, rb, stride=S5_CHUNK), :]
        return jnp.concatenate([xs_ref[j] for j in range(S5_CHUNK)], axis=-1).astype(BF16)

    def scatter_rows(y, y_ref):
        for j in range(S5_CHUNK):
            ys_ref[j] = y[:, j * LANES:(j + 1) * LANES]
        for j in range(S5_CHUNK):
            for b in range(nb):
                y_ref[b, pl.ds(j, rb, stride=S5_CHUNK), :] = ys_ref[j, pl.ds(b, rb, stride=nb), :]

    def scan():
        per = max(1, SUBLANES // nb)
        rows = per * nb
        steps = rb // per
        dirs = ((sf_ref, af_ref, False), (sb_ref, ab_ref, True))
        poles = [(jnp.broadcast_to(a_ref[:, :half], (nb, half)), jnp.broadcast_to(a_ref[:, half:], (nb, half)))
                 for _, a_ref, _ in dirs]

        def body(step, carry):
            out = []
            for (s_ref, _, reverse), (a_re, a_im), (e_re, e_im) in zip(dirs, poles, carry):
                row = ((steps - 1 - step) if reverse else step) * rows
                s = s_ref[row:row + rows, :]
                entering = [None] * per
                for q in (range(per - 1, -1, -1) if reverse else range(per)):
                    sq = s[q * nb:(q + 1) * nb]
                    entering[q] = jnp.concatenate([e_re, e_im], axis=-1)
                    e_re, e_im = (a_re * e_re - a_im * e_im + sq[:, :half],
                                  a_re * e_im + a_im * e_re + sq[:, half:])
                s_ref[row:row + rows, :] = jnp.concatenate(entering, axis=0)
                out.append((e_re, e_im))
            return tuple(out)

        carry = tuple((carry_ref[slot][:, :half], carry_ref[slot][:, half:]) for slot in range(2))
        for step in range(steps):
            carry = body(step, carry)
        for slot, (e_re, e_im) in enumerate(carry):
            carry_ref[slot] = jnp.concatenate([e_re, e_im], axis=-1)

    x_f = gather_rows(xf_ref)
    sf_ref[...] = _dot(x_f, wsf_ref[...])
    x_b = gather_rows(xb_ref)
    sb_ref[...] = _dot(x_b, wsb_ref[...])
    y_intra = _dot(x_f, wi_ref[...])
    scan()
    y_f = y_intra + _dot(sf_ref[...].astype(BF16), wof_ref[...])
    scatter_rows(y_f, yf_ref)
    y_b = _dot(sb_ref[...].astype(BF16), wob_ref[...])
    scatter_rows(y_b, yb_ref)


def _s5(sx, s5w, bt, l, nb):
    lb = min(S5_ROWS * S5_CHUNK // nb, l)
    rb = lb // S5_CHUNK
    nl = l // lb
    sx4 = sx.reshape(SLABS, bt, l, LANES)
    w_in, w_intra, w_out, a = s5w
    wspec = lambda k: pl.BlockSpec((None, None, S5_K, S5_K), lambda s, g, i: (k, s, 0, 0))
    aspec = lambda k: pl.BlockSpec((None, None, 1, 2 * SLAB_STATE), lambda s, g, i: (k, s, 0, 0))
    fwd = pl.BlockSpec((None, nb, lb, LANES), lambda s, g, i: (s, g, i, 0))
    bwd = pl.BlockSpec((None, nb, lb, LANES), lambda s, g, i: (s, g, nl - 1 - i, 0))
    m = rb * nb
    yf, yb = pl.pallas_call(
        functools.partial(_s5_kernel, nb=nb, rb=rb),
        grid=(SLABS, bt // nb, nl),
        in_specs=[fwd, bwd, wspec(0), wspec(1), pl.BlockSpec((None, S5_K, S5_K), lambda s, g, i: (s, 0, 0)),
                  wspec(0), wspec(1), aspec(0), aspec(1)],
        out_specs=[fwd, bwd],
        out_shape=[jax.ShapeDtypeStruct(sx4.shape, F32)] * 2,
        scratch_shapes=[
            pltpu.VMEM((S5_CHUNK, m, LANES), F32),
            pltpu.VMEM((m, 2 * SLAB_STATE), F32),
            pltpu.VMEM((m, 2 * SLAB_STATE), F32),
            pltpu.VMEM((S5_CHUNK, m, LANES), F32),
            pltpu.VMEM((2, nb, 2 * SLAB_STATE), F32),
        ],
        compiler_params=_cparams(("parallel", "parallel", "arbitrary")),
        name="s5",
    )(sx4, sx4, w_in, w_in, w_intra, w_out, w_out, a, a)
    return yf.reshape(sx.shape), yb.reshape(sx.shape)


def _s5_weights(lam_re, lam_im, log_dt, b_re, b_im, c_re, c_im, d):
    hi = lax.Precision.HIGHEST
    tc, ng, ns, nc = S5_CHUNK, SLAB_GROUPS, SSM_STATE, SSM_GROUP
    direction = np.arange(2)[:, None]
    steps = np.arange(tc)
    dt = jnp.exp(log_dt)[..., None]
    mag = jnp.exp(lam_re * dt)
    a_re, a_im = mag * jnp.cos(lam_im * dt), mag * jnp.sin(lam_im * dt)
    den = lam_re * lam_re + lam_im * lam_im
    nr = a_re - 1.0
    k_re = (nr * lam_re + a_im * lam_im) / den
    k_im = (a_im * lam_re - nr * lam_im) / den
    tau = jnp.arange(tc + 1, dtype=F32)[None, :, None, None]
    pmag = jnp.exp((lam_re * dt)[:, None] * tau)
    p_re = pmag * jnp.cos((lam_im * dt)[:, None] * tau)
    p_im = pmag * jnp.sin((lam_im * dt)[:, None] * tau)

    group_of = lambda n, width: (jnp.arange(n, dtype=jnp.int32) // width) % ng
    own = lambda lane_group: jnp.arange(ng, dtype=jnp.int32)[:, None] == lane_group[None, :]

    sel = np.stack([tc - 1 - steps, steps])
    pq_re, pq_im = p_re[direction, sel], p_im[direction, sel]
    q_re = pq_re * k_re[:, None] - pq_im * k_im[:, None]
    q_im = pq_re * k_im[:, None] + pq_im * k_re[:, None]
    to_jlanes = lambda q: jnp.transpose(q.reshape(2, tc, SLABS, ng * ns), (0, 2, 1, 3))[:, :, :, None, None, :]
    to_clanes = lambda b: jnp.transpose(b.reshape(2, SLABS, ng, ns, nc), (0, 1, 4, 2, 3)).reshape(
        2, SLABS, 1, 1, nc, ng * ns)
    qr, qi = to_jlanes(q_re), to_jlanes(q_im)
    br, bi = to_clanes(b_re), to_clanes(b_im)
    keep = own(group_of(ng * ns, ns))[None, None, None, :, None, :]
    w_in = jnp.concatenate([jnp.where(keep, qr * br - qi * bi, 0.0), jnp.where(keep, qr * bi + qi * br, 0.0)],
                           axis=-1).astype(BF16).reshape(2, SLABS, S5_K, S5_K)

    sel = np.stack([steps + 1, tc - steps])
    lane = jnp.arange(S5_K, dtype=jnp.int32)
    spread = (jnp.arange(tc * ng, dtype=jnp.int32)[:, None]
              == ((lane // LANES) * ng + (lane % LANES) // nc)[None, :]).astype(F32)
    to_nrows = lambda p: jnp.dot(
        jnp.transpose(p[direction, sel].reshape(2, tc, SLABS, ng, ns), (0, 2, 4, 1, 3)).reshape(-1, tc * ng),
        spread, precision=hi).reshape(2, SLABS, ns, S5_K)
    to_dlanes = lambda c: jnp.tile(
        jnp.transpose(c.reshape(2, SLABS, ng, nc, ns), (0, 1, 4, 2, 3)).reshape(2, SLABS, ns, LANES), (1, 1, 1, tc))
    pr, pi = to_nrows(p_re), to_nrows(p_im)
    cr, ci = to_dlanes(c_re), to_dlanes(c_im)
    w = jnp.stack([cr * pr - ci * pi, -(cr * pi + ci * pr)], axis=2)
    keep = own(group_of(S5_K, nc))[None, None, None, :, None, :]
    w_out = jnp.where(keep, w[:, :, :, None], 0.0).astype(BF16).reshape(2, SLABS, S5_K, S5_K)

    cp_re = c_re[:, None] * p_re[:, :tc, :, None, :] - c_im[:, None] * p_im[:, :tc, :, None, :]
    cp_im = c_re[:, None] * p_im[:, :tc, :, None, :] + c_im[:, None] * p_re[:, :tc, :, None, :]
    bb_re = k_re[..., None] * b_re - k_im[..., None] * b_im
    bb_im = k_re[..., None] * b_im + k_im[..., None] * b_re
    kern = jnp.einsum('klgdn,kgnc->kgcld', jnp.concatenate([cp_re, -cp_im], axis=-1),
                      jnp.concatenate([bb_re, bb_im], axis=-2), precision=lax.Precision.HIGH)
    kern = jnp.transpose(kern.reshape(2, SLABS, ng, nc, tc, nc), (0, 1, 4, 3, 2, 5)).reshape(
        2, SLABS, tc, nc, LANES)
    chan = jnp.arange(nc, dtype=jnp.int32)
    skip = jnp.where(chan[:, None] == (jnp.arange(LANES, dtype=jnp.int32) % nc)[None, :],
                     d.reshape(SLABS, 1, LANES), 0.0)
    centre = (kern[0, :, 0] + kern[1, :, 0] + skip)[:, None]
    lags = jnp.concatenate([kern[1, :, :0:-1], centre, kern[0, :, 1:]], axis=1)
    rows = jnp.stack([jnp.concatenate([lags[:, tc - 1 + jp - j] for jp in range(tc)], axis=-1)
                      for j in range(tc)], axis=1)
    keep = own(group_of(S5_K, nc))[None, None, :, None, :]
    w_intra = jnp.where(keep, rows[:, :, None], 0.0).astype(BF16).reshape(SLABS, S5_K, S5_K)

    a = jnp.stack([p_re[:, tc], p_im[:, tc]], axis=1).reshape(2, 2, SLABS, SLAB_STATE)
    a = jnp.transpose(a, (0, 2, 1, 3)).reshape(2, SLABS, 1, 2 * SLAB_STATE)
    return w_in, w_intra, w_out, a


def _rel_bucket(rel):
    half = NUM_BUCKETS // 2
    max_exact = half // 2
    ret = (rel > 0).astype(jnp.int32) * half
    n = jnp.abs(rel)
    nf = jnp.maximum(n, 1).astype(jnp.float32)
    large = max_exact + (jnp.log(nf / max_exact) / math.log(MAX_DISTANCE / max_exact)
                         * (half - max_exact)).astype(jnp.int32)
    large = jnp.minimum(large, half - 1)
    return ret + jnp.where(n < max_exact, n, large)


def _bias_tiles(rel_bias, tile):
    assert tile >= MAX_DISTANCE
    qi = jnp.arange(tile, dtype=jnp.int32)[:, None]
    kj = jnp.arange(tile, dtype=jnp.int32)[None, :]
    offs = (jnp.arange(5, dtype=jnp.int32) - 2) * tile
    rel = offs[:, None, None] + kj[None] - qi[None]
    bucket = _rel_bucket(rel)[None]
    table = rel_bias.astype(F32) * LOG2E
    tab = jnp.zeros((ATT_HEADS,) + rel.shape, F32)
    for n in range(NUM_BUCKETS):
        tab = jnp.where(bucket == n, table[n][:, None, None, None], tab)
    return tab


def _attn_kernel(lam_ref, q_ref, k_ref, v_ref, bias_ref, g_ref, o_ref,
                 sa_ref, sb_ref, ma_ref, mb_ref, *, tile, nk):
    hd = ATT_HEAD_DIM
    nt = (((1,), (1,)), ((), ()))
    lane_blocks = tile // LANES
    reps = ATT_V_DIM // LANES
    nq = nk
    assert nq % 2 == 0

    def tile_rows(i):
        return pl.ds(pl.multiple_of(i * tile, tile), tile)

    def step(i_qk, i_pv, w_bufs, r_bufs):
        if i_qk is not None:
            s_w, m_w = w_bufs
            q_next = [q_ref[tile_rows(i_qk), c * hd:(c + 1) * hd] for c in range(2)]
            mx = [jnp.full((tile, LANES), -jnp.inf, F32) for _ in range(2)]
        if i_pv is not None:
            s_r, m_r = r_bufs
            ls = [jnp.zeros((tile, LANES), F32) for _ in range(2)]
            acc = [None, None]
        for kj in range(nk):
            cols = slice(kj * tile, (kj + 1) * tile)
            if i_qk is not None:
                bias = bias_ref[jnp.clip(kj - i_qk, -2, 2) + 2]
                for c in range(2):
                    s = lax.dot_general(q_next[c], k_ref[cols, c * hd:(c + 1) * hd], nt,
                                        preferred_element_type=F32) + bias
                    s_w[c, :, cols] = s
                    mx[c] = jnp.maximum(mx[c], functools.reduce(
                        jnp.maximum, [s[:, b * LANES:(b + 1) * LANES] for b in range(lane_blocks)]))
            if i_pv is not None:
                vt = v_ref[cols, :]
                for c in range(2):
                    ps = []
                    for b in range(lane_blocks):
                        lo = kj * tile + b * LANES
                        ps.append(jnp.exp2(s_r[c, :, lo:lo + LANES] - m_r[c]))
                    ls[c] = ls[c] + functools.reduce(jnp.add, ps)
                    d = _dot(jnp.concatenate([p.astype(BF16) for p in ps], axis=1), vt)
                    acc[c] = d if acc[c] is None else acc[c] + d
        if i_qk is not None:
            for c in range(2):
                m_w[c] = jnp.broadcast_to(jnp.max(mx[c], axis=1, keepdims=True), (tile, LANES))
        if i_pv is not None:
            inv0 = jnp.broadcast_to(1.0 / jnp.sum(ls[0], axis=1, keepdims=True), (tile, LANES))
            inv1 = jnp.broadcast_to(lam_ref[0] / jnp.sum(ls[1], axis=1, keepdims=True), (tile, LANES))
            o = (acc[0] * jnp.concatenate([inv0] * reps, axis=1)
                 - acc[1] * jnp.concatenate([inv1] * reps, axis=1))
            ms = jnp.mean(o * o, axis=-1, keepdims=True)
            o_ref[tile_rows(i_pv), :] = (o * lax.rsqrt(ms + EPS) * g_ref[...] * (1.0 - LAM_INIT)).astype(BF16)

    buf_a = (sa_ref, ma_ref)
    buf_b = (sb_ref, mb_ref)
    step(0, None, buf_a, None)

    def pair(j, carry):
        step(2 * j + 1, 2 * j, buf_b, buf_a)
        step(2 * j + 2, 2 * j + 1, buf_a, buf_b)
        return carry

    lax.fori_loop(0, nq // 2 - 1, pair, 0)
    step(nq - 1, nq - 2, buf_b, buf_a)
    step(None, nq - 1, None, buf_b)


def _attention(proj3, bias_tiles, subln_g, lam, tile):
    bt, l, _ = proj3.shape
    nk = l // tile
    blk = ATT_V_DIM
    seq = lambda c0: pl.BlockSpec((None, l, blk), lambda h, b: (b, 0, c0 // blk + h))
    return pl.pallas_call(
        functools.partial(_attn_kernel, tile=tile, nk=nk),
        grid=(ATT_HEADS, bt),
        in_specs=[
            pl.BlockSpec(memory_space=pltpu.SMEM),
            seq(COL_Q), seq(COL_K), seq(COL_V),
            pl.BlockSpec((None, 5, tile, tile), lambda h, b: (h, 0, 0, 0), pipeline_mode=pl.Buffered(1)),
            pl.BlockSpec((1, blk), lambda h, b: (0, 0)),
        ],
        out_specs=seq(0),
        out_shape=jax.ShapeDtypeStruct((bt, l, ATT_WIDTH), BF16),
        scratch_shapes=[
            pltpu.VMEM((2, tile, l), F32),
            pltpu.VMEM((2, tile, l), F32),
            pltpu.VMEM((2, tile, LANES), F32),
            pltpu.VMEM((2, tile, LANES), F32),
        ],
        compiler_params=_cparams(("parallel", "parallel")),
        name="attention",
    )(lam, proj3, proj3, proj3, bias_tiles, subln_g)


def _branches_kernel(yf_ref, yb_ref, sz_ref, az_ref, og_ref, gs_ref, ga_ref, gw_ref, gb_ref, ws_ref, wa_ref,
                     o_ref):
    half = o_ref.shape[0] // 2
    for r in (slice(0, half), slice(half, 2 * half)):
        y = jnp.concatenate([yf_ref[s, r, :] + yb_ref[s, r, :] for s in range(SLABS)], axis=-1)
        y = jax.nn.gelu(y)
        y = y * _sigmoid(_dot(y.astype(BF16), gw_ref[...]) + gb_ref[...])
        sz = sz_ref[r, :].astype(F32)
        y = (y * (sz * _sigmoid(sz))).astype(BF16)
        y_s = _dot(y, ws_ref[...])
        az = az_ref[r, :].astype(F32)
        o = (og_ref[r, :].astype(F32) * (az * _sigmoid(az))).astype(BF16)
        y_a = _dot(o, wa_ref[...])
        g_s = gs_ref[r, :].astype(F32)
        g_a = ga_ref[r, :].astype(F32)
        o_ref[r, :] = (_sigmoid(g_s) * y_s + _sigmoid(g_a) * y_a).astype(BF16)


def _branches(yf, yb, proj, og, glu_w, glu_b, w_s, w_a, tm):
    t = proj.shape[0]
    slab = pl.BlockSpec((SLABS, tm, LANES), lambda i: (0, i, 0))
    cols = lambda c0, width: pl.BlockSpec((tm, width), lambda i: (i, c0 // width))
    return pl.pallas_call(
        _branches_kernel,
        grid=(t // tm,),
        in_specs=[
            slab, slab,
            cols(COL_SZ, SSM_WIDTH), cols(COL_AZ, ATT_WIDTH),
            pl.BlockSpec((tm, ATT_WIDTH), lambda i: (i, 0)),
            cols(COL_GS, D_MODEL), cols(COL_GA, D_MODEL),
            _const_spec((SSM_WIDTH, SSM_WIDTH)),
            _const_spec((1, SSM_WIDTH)),
            _const_spec((SSM_WIDTH, D_MODEL)),
            _const_spec((ATT_WIDTH, D_MODEL)),
        ],
        out_specs=pl.BlockSpec((tm, D_MODEL), lambda i: (i, 0)),
        out_shape=jax.ShapeDtypeStruct((t, D_MODEL), BF16),
        compiler_params=_cparams(("parallel",)),
        name="branches",
    )(yf, yb, proj, proj, og, proj, proj, glu_w, glu_b, w_s, w_a)


def _rms(x, g):
    ms = jnp.mean(x * x, axis=-1, keepdims=True)
    return x * lax.rsqrt(ms + EPS) * g


def _tail_kernel(m_ref, x_ref, p_ref, wo_ref, png_ref, pgw_ref, ppw_ref, fg_ref, o_ref):
    h = x_ref[...] + _dot(m_ref[...], wo_ref[...])
    gate = _sigmoid(_dot(_rms(h, png_ref[...]).astype(BF16), pgw_ref[...]))
    h = h + gate * _dot(p_ref[...].astype(BF16), ppw_ref[...])
    o_ref[...] = _rms(h, fg_ref[...])


def _tail(merged, x2d, p2d, w_out, ple_norm_g, ple_gate_w, ple_proj_w, final_g, tm):
    t = x2d.shape[0]
    row = lambda w: pl.BlockSpec((tm, w), lambda i: (i, 0))
    return pl.pallas_call(
        _tail_kernel,
        grid=(t // tm,),
        in_specs=[
            row(D_MODEL), row(D_MODEL), row(PLE_DIM),
            _const_spec((D_MODEL, D_MODEL)),
            _const_spec((1, D_MODEL)),
            _const_spec((D_MODEL, D_MODEL)),
            _const_spec((PLE_DIM, D_MODEL)),
            _const_spec((1, D_MODEL)),
        ],
        out_specs=row(D_MODEL),
        out_shape=jax.ShapeDtypeStruct((t, D_MODEL), F32),
        compiler_params=_cparams(("parallel",)),
        name="tail",
    )(merged, x2d, p2d, w_out, ple_norm_g, ple_gate_w, ple_proj_w, final_g)


def _tiles(bt, l):
    t = bt * l
    tm_in = 1024 if t % 1024 == 0 else l
    tm = 512 if t % 512 == 0 else l
    nb = 8 if bt % 8 == 0 else (4 if bt % 4 == 0 else (2 if bt % 2 == 0 else 1))
    att = 256 if l % 256 == 0 else MAX_DISTANCE
    return tm_in, tm, nb, att


def _trunk(x, p, w):
    bt, l, _ = x.shape
    t = bt * l
    tm_in, tm, nb, att_tile = _tiles(bt, l)
    x2d = x.reshape(t, D_MODEL)
    proj, sx = _inproj(x2d, w['norm_g'], w['w_in'], tm_in)
    yf, yb = _s5(sx, w['s5'], bt, l, nb)
    og = _attention(proj.reshape(bt, l, PROJ_W), _bias_tiles(w['rel_bias'], att_tile), w['subln_g'],
                    w['lam'], att_tile)
    merged = _branches(yf, yb, proj, og.reshape(t, ATT_WIDTH), w['glu_w'], w['glu_b'],
                       w['w_branch_s'], w['w_branch_a'], tm)
    out = _tail(merged, x2d, p.reshape(t, PLE_DIM), w['w_out'], w['ple_norm_g'], w['ple_gate_w'],
                w['ple_proj_w'], w['final_g'], tm)
    return out.reshape(bt, l, D_MODEL)


def _prep_weights(rel_bias, norm_g, w_in, ssm_lambda_re, ssm_lambda_im, ssm_log_dt, ssm_b_re, ssm_b_im,
                  ssm_c_re, ssm_c_im, ssm_d, glu_w, glu_b, lam_q1, lam_k1, lam_q2, lam_k2, subln_g,
                  w_branch_s, w_branch_a, w_out, ple_norm_g, ple_gate_w, ple_proj_w, final_g):
    lam = (jnp.exp(jnp.sum(lam_q1[0] * lam_k1[0])) - jnp.exp(jnp.sum(lam_q2[0] * lam_k2[0])) + LAM_INIT)
    return {
        'rel_bias': rel_bias,
        'norm_g': norm_g[0].reshape(1, D_MODEL),
        'w_in': w_in[0].astype(BF16),
        's5': _s5_weights(ssm_lambda_re[0], ssm_lambda_im[0], ssm_log_dt[0], ssm_b_re[0], ssm_b_im[0],
                          ssm_c_re[0], ssm_c_im[0], ssm_d[0]),
        'glu_w': glu_w[0].astype(BF16),
        'glu_b': glu_b[0].reshape(1, SSM_WIDTH),
        'lam': lam.reshape(1).astype(F32),
        'subln_g': subln_g[0].reshape(1, ATT_V_DIM),
        'w_branch_s': w_branch_s[0].astype(BF16),
        'w_branch_a': w_branch_a[0].astype(BF16),
        'w_out': w_out[0].astype(BF16),
        'ple_norm_g': ple_norm_g[0].reshape(1, D_MODEL),
        'ple_gate_w': ple_gate_w[0].astype(BF16),
        'ple_proj_w': ple_proj_w[0].astype(BF16),
        'final_g': final_g.reshape(1, D_MODEL),
    }


def kernel(x_prompt, x_sample, p_prompt, p_sample, rel_bias, norm_g, w_in, ssm_lambda_re, ssm_lambda_im,
           ssm_log_dt, ssm_b_re, ssm_b_im, ssm_c_re, ssm_c_im, ssm_d, glu_w, glu_b, lam_q1, lam_k1, lam_q2,
           lam_k2, subln_g, w_branch_s, w_branch_a, w_out, ple_norm_g, ple_gate_w, ple_proj_w, final_g):
    w = _prep_weights(rel_bias, norm_g, w_in, ssm_lambda_re, ssm_lambda_im, ssm_log_dt, ssm_b_re, ssm_b_im,
                      ssm_c_re, ssm_c_im, ssm_d, glu_w, glu_b, lam_q1, lam_k1, lam_q2, lam_k2, subln_g,
                      w_branch_s, w_branch_a, w_out, ple_norm_g, ple_gate_w, ple_proj_w, final_g)
    return (_trunk(x_prompt, p_prompt[0], w), _trunk(x_sample, p_sample[0], w))
```
